```python
import math
import jax
import jax.numpy as jnp
from jax import lax
import numpy as np

D_MODEL = 1024
BATCH = 1
SEQ = 16384
DEPTH = 1
DEC_BATCH = 128
DEC_SEQ = 1
PAST_LEN = 8192
PAGE_SIZE = 128

RWKV_HEADS = 8
RWKV_HEAD = 64
RWKV_W = RWKV_HEADS * RWKV_HEAD
DECAY_LORA = 64
AAA_LORA = 64
GATE_LORA = 128
RWKV_PROJ = 3 * RWKV_W + DECAY_LORA + AAA_LORA + GATE_LORA
RWKV_LN_EPS = 64e-5
ATT_HEADS = 4
HEAD_QK = 64
HEAD_V = 2 * HEAD_QK
ATT_QK_W = ATT_HEADS * 2 * HEAD_QK
ATT_V_W = ATT_HEADS * HEAD_V
ATT_PROJ = 2 * ATT_QK_W + ATT_V_W
Q_BLOCK = 128
SUBLN_EPS = 1e-5
N_BUCKETS = 32
MAX_EXACT = N_BUCKETS // 2
MAX_DISTANCE = 128
GATE_PROJ = 2 * D_MODEL
PROJ_TOTAL = RWKV_PROJ + ATT_PROJ + GATE_PROJ
PEER_HEADS = 8
PEER_QDIM = 256
PEER_HALF = PEER_QDIM // 2
N_KEYS = 128
N_EXPERTS = N_KEYS * N_KEYS
PEER_TOPK = 16
PEER_BLOCK = 128
PLE_DIM = 256
NORM_EPS = 1e-6
NEG_INF = -1e30

kernel_name = 'rwkv7_diffattn_peer_hybrid_step'


def rmsnorm(x, g, eps=NORM_EPS):
    xf = x.astype(jnp.float32)
    y = xf * lax.rsqrt(jnp.mean(xf * xf, axis=-1, keepdims=True) + eps)
    return (y * g.astype(jnp.float32)).astype(x.dtype)


def lambda_init(layer):
    return 0.8 - 0.6 * math.exp(-0.3 * layer)


def t5_bias(table, q_pos, k_pos):
    dist = q_pos[:, None] - k_pos[None, :]
    valid = dist >= 0
    n = jnp.maximum(dist, 0)
    nf = jnp.maximum(n, 1).astype(jnp.float32)
    large = MAX_EXACT + (jnp.log(nf / MAX_EXACT) / math.log(MAX_DISTANCE / MAX_EXACT)
                         * (N_BUCKETS - MAX_EXACT)).astype(jnp.int32)
    large = jnp.minimum(large, N_BUCKETS - 1)
    bucket = jnp.where(n < MAX_EXACT, n, large)
    bias = jnp.transpose(table.astype(jnp.float32)[bucket], (2, 0, 1))
    return bias, valid


def diff_softmax(s, bias, valid, lam):
    s = s + bias[None, :, None]
    s = jnp.where(valid[None, None, None], s, NEG_INF)
    p = jax.nn.softmax(s, axis=-1)
    return p[:, :, 0] - lam * p[:, :, 1]


def attn_prompt(q, k, v, table, lam):
    B, S = q.shape[0], q.shape[1]
    scale = HEAD_QK ** -0.5
    kf = k.astype(jnp.float32)
    vf = v.astype(jnp.float32)
    k_pos = jnp.arange(S)

    def block(i):
        qb = lax.dynamic_slice_in_dim(q, i * Q_BLOCK, Q_BLOCK, axis=1).astype(jnp.float32)
        s = jnp.einsum('bqhmd,bkhmd->bhmqk', qb, kf) * scale
        bias, valid = t5_bias(table, i * Q_BLOCK + jnp.arange(Q_BLOCK), k_pos)
        a = diff_softmax(s, bias, valid, lam)
        return jnp.einsum('bhqk,bkhd->bqhd', a, vf)

    o = lax.map(block, jnp.arange(S // Q_BLOCK))
    return jnp.transpose(o, (1, 0, 2, 3, 4)).reshape(B, S, ATT_HEADS, HEAD_V)


def attn_sample(q, k, v, cache_k, cache_v, layer, page_table, table, lam):
    DB, DS = q.shape[0], q.shape[1]
    past = page_table.shape[1] * PAGE_SIZE
    scale = HEAD_QK ** -0.5
    kp = cache_k[layer, page_table].reshape(DB, past, ATT_HEADS, 2, HEAD_QK).astype(jnp.float32)
    vp = cache_v[layer, page_table].reshape(DB, past, ATT_HEADS, HEAD_V).astype(jnp.float32)
    qf = q.astype(jnp.float32)
    s = jnp.concatenate([jnp.einsum('bqhmd,bkhmd->bhmqk', qf, kp),
                         jnp.einsum('bqhmd,bkhmd->bhmqk', qf, k.astype(jnp.float32))], axis=-1) * scale
    bias, valid = t5_bias(table, past + jnp.arange(DS), jnp.arange(past + DS))
    a = diff_softmax(s, bias, valid, lam)
    return (jnp.einsum('bhqk,bkhd->bqhd', a[..., :past], vp)
            + jnp.einsum('bhqk,bkhd->bqhd', a[..., past:], v.astype(jnp.float32)))


def rwkv_branch(z, shift0, wkv0, W):
    B, T = z.shape[0], z.shape[1]
    z_prev = jnp.concatenate([shift0[:, None].astype(z.dtype), z[:, :-1]], axis=1)
    zs = z + (z_prev - z) * W['rwkv_mu']
    o = 0
    r = zs[..., o:o + RWKV_W]; o += RWKV_W
    k = zs[..., o:o + RWKV_W]; o += RWKV_W
    v = zs[..., o:o + RWKV_W]; o += RWKV_W
    w_lo = zs[..., o:o + DECAY_LORA]; o += DECAY_LORA
    a_lo = zs[..., o:o + AAA_LORA]; o += AAA_LORA
    g_lo = zs[..., o:o + GATE_LORA]
    w = -jax.nn.softplus(-(W['rwkv_w0'] + jnp.tanh(w_lo) @ W['rwkv_w2'])) - 0.5
    decay = jnp.exp(-jnp.exp(w.astype(jnp.float32)))
    a = jax.nn.sigmoid(W['rwkv_a0'] + a_lo @ W['rwkv_a2'])
    g = jax.nn.sigmoid(g_lo) @ W['rwkv_g2']

    def heads(t):
        return t.reshape(B, T, RWKV_HEADS, RWKV_HEAD).astype(jnp.float32)

    kk = heads(k * W['rwkv_k_k'])
    kk = kk / jnp.maximum(jnp.sqrt(jnp.sum(kk * kk, axis=-1, keepdims=True)), 1e-12)
    k = k * (1.0 + (a - 1.0) * W['rwkv_k_a'])
    rh, kh, vh, ah, dh = heads(r), heads(k), heads(v), heads(a), heads(decay)

    def step(S, inp):
        r_t, d_t, k_t, v_t, kk_t, a_t = inp
        sa = jnp.einsum('bhvk,bhk->bhv', S, -kk_t)
        S = (S * d_t[:, :, None, :] + sa[..., None] * (kk_t * a_t)[:, :, None, :]
             + v_t[..., None] * k_t[:, :, None, :])
        return S, jnp.einsum('bhvk,bhk->bhv', S, r_t)

    tm = lambda t: jnp.swapaxes(t, 0, 1)
    wkv_T, y = lax.scan(step, wkv0.astype(jnp.float32), (tm(rh), tm(dh), tm(kh), tm(vh), tm(kk), tm(ah)))
    y = tm(y)
    mean = jnp.mean(y, axis=-1, keepdims=True)
    var = jnp.mean(jnp.square(y - mean), axis=-1, keepdims=True)
    y = ((y - mean) * lax.rsqrt(var + RWKV_LN_EPS)).reshape(B, T, RWKV_W)
    y = y * W['rwkv_ln_w'].astype(jnp.float32) + W['rwkv_ln_b'].astype(jnp.float32)
    bonus = jnp.sum(rh * kh * W['rwkv_r_k'].astype(jnp.float32), axis=-1, keepdims=True) * vh
    y = (y + bonus.reshape(B, T, RWKV_W)) * g.astype(jnp.float32)
    return y.astype(z.dtype), wkv_T, z[:, -1]


def peer_ffn(xn, wq, sk1, sk2, eu, ev):
    B, T, D = xn.shape
    n = B * T
    nb = -(-n // PEER_BLOCK)
    xf = jnp.pad(xn.reshape(n, D), ((0, nb * PEER_BLOCK - n), (0, 0)))

    def block(xb):
        q = (xb @ wq).reshape(PEER_BLOCK, PEER_HEADS, PEER_QDIM)
        s1 = jnp.einsum('thd,nd->thn', q[..., :PEER_HALF], sk1).astype(jnp.float32)
        s2 = jnp.einsum('thd,nd->thn', q[..., PEER_HALF:], sk2).astype(jnp.float32)
        v1, i1 = lax.top_k(s1, PEER_TOPK)
        v2, i2 = lax.top_k(s2, PEER_TOPK)
        cand = (v1[..., :, None] + v2[..., None, :]).reshape(PEER_BLOCK, PEER_HEADS, PEER_TOPK * PEER_TOPK)
        cidx = (i1[..., :, None] * N_KEYS + i2[..., None, :]).reshape(PEER_BLOCK, PEER_HEADS, PEER_TOPK * PEER_TOPK)
        top, pos = lax.top_k(cand, PEER_TOPK)
        eidx = jnp.take_along_axis(cidx, pos, axis=-1)
        gate = jax.nn.softmax(top, axis=-1)
        act = jax.nn.gelu(jnp.einsum('thkd,td->thk', eu[eidx], xb).astype(jnp.float32), approximate=False)
        return jnp.einsum('thk,thkd->td', (gate * act).astype(xb.dtype), ev[eidx])

    y = lax.map(block, xf.reshape(nb, PEER_BLOCK, D))
    return y.reshape(nb * PEER_BLOCK, D)[:n].reshape(B, T, D)


def trunk_layer(x, pe, W, lam_init, shift0, wkv0, attend):
    B, T = x.shape[0], x.shape[1]
    h = rmsnorm(x, W['norm_mix'])
    z = h @ W['w_in']
    o = RWKV_PROJ
    z_rwkv = z[..., :o]
    q = z[..., o:o + ATT_QK_W].reshape(B, T, ATT_HEADS, 2, HEAD_QK); o += ATT_QK_W
    k = z[..., o:o + ATT_QK_W].reshape(B, T, ATT_HEADS, 2, HEAD_QK); o += ATT_QK_W
    v = z[..., o:o + ATT_V_W].reshape(B, T, ATT_HEADS, HEAD_V); o += ATT_V_W
    gate_rwkv = jax.nn.sigmoid(z[..., o:o + D_MODEL]); o += D_MODEL
    gate_attn = jax.nn.sigmoid(z[..., o:o + D_MODEL])
    y_rwkv, wkv_T, shift_T = rwkv_branch(z_rwkv, shift0, wkv0, W)
    y_attn = attend(q, k, v)
    y_attn = (rmsnorm(y_attn, W['attn_subln'], SUBLN_EPS) * (1.0 - lam_init)).reshape(B, T, ATT_V_W)
    merged = gate_rwkv * (y_rwkv @ W['w_br_rwkv']) + gate_attn * (y_attn @ W['w_br_attn'])
    x = x + merged @ W['w_out']
    x = x + peer_ffn(rmsnorm(x, W['norm_ffn']), W['peer_wq'], W['peer_k1'], W['peer_k2'], W['peer_u'], W['peer_v'])
    x = x + (pe @ W['w_ple']) * jax.nn.sigmoid(rmsnorm(x, W['norm_ple']) @ W['w_ple_gate'])
    return x, k.reshape(B, T, ATT_HEADS, 2 * HEAD_QK), v, wkv_T, shift_T


def setup_inputs(seed: int = 0) -> dict:
    key = jax.random.key(seed)
    ks = iter(jax.random.split(key, 64))
    f32 = jnp.float32
    nrm = lambda shape, scale: jax.random.normal(next(ks), shape, f32) * scale
    uni = lambda shape, lo, hi: jax.random.uniform(next(ks), shape, f32, lo, hi)
    n_pages = PAST_LEN // PAGE_SIZE
    used = DEC_BATCH * n_pages
    n_phys = used + -(-used // 4)
    page_table = jax.random.permutation(next(ks), n_phys)[:used].reshape(DEC_BATCH, n_pages).astype(jnp.int32)
    d = D_MODEL
    return {
        'x_prompt': nrm((BATCH, SEQ, d), 1.0),
        'x_sample': nrm((DEC_BATCH, DEC_SEQ, d), 1.0),
        'cache_k': nrm((DEPTH, n_phys, PAGE_SIZE, ATT_HEADS, 2 * HEAD_QK), 1.0),
        'cache_v': nrm((DEPTH, n_phys, PAGE_SIZE, ATT_HEADS, HEAD_V), 1.0),
        'state_wkv': nrm((DEPTH, DEC_BATCH, RWKV_HEADS, RWKV_HEAD, RWKV_HEAD), 0.3),
        'state_shift': nrm((DEPTH, DEC_BATCH, RWKV_PROJ), 1.0),
        'page_table': page_table,
        'p_prompt': nrm((DEPTH, BATCH, SEQ, PLE_DIM), 1.0),
        'p_sample': nrm((DEPTH, DEC_BATCH, DEC_SEQ, PLE_DIM), 1.0),
        'norm_mix': 1.0 + nrm((DEPTH, d), 0.05),
        'w_in': nrm((DEPTH, d, PROJ_TOTAL), d ** -0.5),
        'rwkv_mu': uni((DEPTH, RWKV_PROJ), 0.1, 0.9),
        'rwkv_w0': uni((DEPTH, RWKV_W), -6.0, 0.0),
        'rwkv_w2': nrm((DEPTH, DECAY_LORA, RWKV_W), 0.1 * DECAY_LORA ** -0.5),
        'rwkv_a0': nrm((DEPTH, RWKV_W), 0.1),
        'rwkv_a2': nrm((DEPTH, AAA_LORA, RWKV_W), 0.3 * AAA_LORA ** -0.5),
        'rwkv_g2': nrm((DEPTH, GATE_LORA, RWKV_W), GATE_LORA ** -0.5),
        'rwkv_k_k': 0.85 + nrm((DEPTH, RWKV_W), 0.05),
        'rwkv_k_a': 1.0 + nrm((DEPTH, RWKV_W), 0.05),
        'rwkv_r_k': nrm((DEPTH, RWKV_HEADS, RWKV_HEAD), 0.1),
        'rwkv_ln_w': 1.0 + nrm((DEPTH, RWKV_W), 0.05),
        'rwkv_ln_b': nrm((DEPTH, RWKV_W), 0.02),
        'attn_lq1': nrm((DEPTH, HEAD_QK), 0.1),
        'attn_lk1': nrm((DEPTH, HEAD_QK), 0.1),
        'attn_lq2': nrm((DEPTH, HEAD_QK), 0.1),
        'attn_lk2': nrm((DEPTH, HEAD_QK), 0.1),
        'attn_subln': 1.0 + nrm((DEPTH, HEAD_V), 0.05),
        'rel_bias': nrm((N_BUCKETS, ATT_HEADS), 0.5),
        'w_br_rwkv': nrm((DEPTH, RWKV_W, d), RWKV_W ** -0.5),
        'w_br_attn': nrm((DEPTH, ATT_V_W, d), ATT_V_W ** -0.5),
        'w_out': nrm((DEPTH, d, d), 0.5 * d ** -0.5),
        'norm_ffn': 1.0 + nrm((DEPTH, d), 0.05),
        'peer_wq': nrm((DEPTH, d, PEER_HEADS * PEER_QDIM), d ** -0.5),
        'peer_k1': nrm((DEPTH, N_KEYS, PEER_HALF), PEER_HALF ** -0.5),
        'peer_k2': nrm((DEPTH, N_KEYS, PEER_HALF), PEER_HALF ** -0.5),
        'peer_u': nrm((DEPTH, N_EXPERTS, d), d ** -0.5),
        'peer_v': nrm((DEPTH, N_EXPERTS, d), 0.5),
        'norm_ple': 1.0 + nrm((DEPTH, d), 0.05),
        'w_ple': nrm((DEPTH, PLE_DIM, d), 0.5 * PLE_DIM ** -0.5),
        'w_ple_gate': nrm((DEPTH, d, d), d ** -0.5),
        'norm_final': 1.0 + nrm((d,), 0.05),
    }


def reference(x_prompt, x_sample, cache_k, cache_v, state_wkv, state_shift, page_table, p_prompt, p_sample,
              norm_mix, w_in, rwkv_mu, rwkv_w0, rwkv_w2, rwkv_a0, rwkv_a2, rwkv_g2, rwkv_k_k, rwkv_k_a,
              rwkv_r_k, rwkv_ln_w, rwkv_ln_b, attn_lq1, attn_lk1, attn_lq2, attn_lk2, attn_subln, rel_bias,
              w_br_rwkv, w_br_attn, w_out, norm_ffn, peer_wq, peer_k1, peer_k2, peer_u, peer_v,
              norm_ple, w_ple, w_ple_gate, norm_final):
    xp, xs = x_prompt, x_sample
    B, DB = xp.shape[0], xs.shape[0]
    kps, vps, kss, vss, wps, wss, hps, hss = [], [], [], [], [], [], [], []
    for l in range(DEPTH):
        W = {
            'norm_mix': norm_mix[l], 'w_in': w_in[l], 'rwkv_mu': rwkv_mu[l], 'rwkv_w0': rwkv_w0[l],
            'rwkv_w2': rwkv_w2[l], 'rwkv_a0': rwkv_a0[l], 'rwkv_a2': rwkv_a2[l], 'rwkv_g2': rwkv_g2[l],
            'rwkv_k_k': rwkv_k_k[l], 'rwkv_k_a': rwkv_k_a[l], 'rwkv_r_k': rwkv_r_k[l],
            'rwkv_ln_w': rwkv_ln_w[l], 'rwkv_ln_b': rwkv_ln_b[l], 'attn_subln': attn_subln[l],
            'w_br_rwkv': w_br_rwkv[l], 'w_br_attn': w_br_attn[l], 'w_out': w_out[l], 'norm_ffn': norm_ffn[l],
            'peer_wq': peer_wq[l], 'peer_k1': peer_k1[l], 'peer_k2': peer_k2[l], 'peer_u': peer_u[l],
            'peer_v': peer_v[l], 'norm_ple': norm_ple[l], 'w_ple': w_ple[l], 'w_ple_gate': w_ple_gate[l],
        }
        lam_init = lambda_init(l)
        lam = (jnp.exp(jnp.sum(attn_lq1[l].astype(jnp.float32) * attn_lk1[l].astype(jnp.float32)))
               - jnp.exp(jnp.sum(attn_lq2[l].astype(jnp.float32) * attn_lk2[l].astype(jnp.float32)))
               + lam_init)

        def attend_prompt(q, k, v, lam=lam):
            return attn_prompt(q, k, v, rel_bias, lam)

        def attend_sample(q, k, v, lam=lam, layer=l):
            return attn_sample(q, k, v, cache_k, cache_v, layer, page_table, rel_bias, lam)

        xp, kp, vp, wp, hp = trunk_layer(
            xp, p_prompt[l], W, lam_init, jnp.zeros((B, RWKV_PROJ), xp.dtype),
            jnp.zeros((B, RWKV_HEADS, RWKV_HEAD, RWKV_HEAD), jnp.float32), attend_prompt)
        xs, ks_, vs_, ws_, hs_ = trunk_layer(
            xs, p_sample[l], W, lam_init, state_shift[l], state_wkv[l], attend_sample)
        kps.append(kp); vps.append(vp); kss.append(ks_); vss.append(vs_)
        wps.append(wp); wss.append(ws_); hps.append(hp); hss.append(hs_)
    y_prompt = rmsnorm(xp, norm_final)
    y_sample = rmsnorm(xs, norm_final)
    return (y_prompt, y_sample, jnp.stack(kps), jnp.stack(vps), jnp.stack(kss), jnp.stack(vss),
            jnp.stack(wps), jnp.stack(wss), jnp.stack(hps), jnp.stack(hss))
```

```python
import functools
import math

import jax
import jax.numpy as jnp
from jax import lax
from jax.experimental import pallas as pl
from jax.experimental.pallas import tpu as pltpu

F32 = jnp.float32
BF16 = jnp.bfloat16
I32 = jnp.int32

D_MODEL = 1024
PAGE_SIZE = 128
RWKV_HEADS = 8
RWKV_HEAD = 64
RWKV_W = RWKV_HEADS * RWKV_HEAD
DECAY_LORA = 64
AAA_LORA = 64
GATE_LORA = 128
RWKV_PROJ = 3 * RWKV_W + DECAY_LORA + AAA_LORA + GATE_LORA
RWKV_LN_EPS = 64e-5
ATT_HEADS = 4
HEAD_QK = 64
HEAD_V = 2 * HEAD_QK
ATT_QK_W = ATT_HEADS * 2 * HEAD_QK
ATT_V_W = ATT_HEADS * HEAD_V
SUBLN_EPS = 1e-5
N_BUCKETS = 32
MAX_EXACT = N_BUCKETS // 2
MAX_DISTANCE = 128
PEER_HEADS = 8
PEER_QDIM = 256
PEER_HALF = PEER_QDIM // 2
N_KEYS = 128
N_EXPERTS = N_KEYS * N_KEYS
PEER_TOPK = 16
PLE_DIM = 256
NORM_EPS = 1e-6
NEG_INF = -1e30

LANES = 128
QUAD = 4 * RWKV_HEAD
CHUNK = 64
VMEM_LIMIT = 56 * 1024 * 1024

NN = (((1,), (0,)), ((), ()))
NT = (((1,), (1,)), ((), ()))
TN = (((0,), (0,)), ((), ()))


def _dot(a, b, dims=NN):
    return lax.dot_general(a, b, dims, preferred_element_type=F32)


def _split(x, n):
    parts = []
    for _ in range(n):
        p = x.astype(BF16)
        parts.append(p)
        x = x - p.astype(F32)
    return parts


def _mm(a, b, dims=NN, passes=1):
    if passes == 1:
        return _dot(a.astype(BF16), b.astype(BF16), dims)
    ah, al = _split(a, 2)
    bh, bl = _split(b, 2)
    return _dot(ah, bh, dims) + (_dot(ah, bl, dims) + _dot(al, bh, dims))


def _mm_exact(a, b, dims=NN):
    a1, a2, a3 = _split(a, 3)
    return _dot(a1, b, dims) + (_dot(a2, b, dims) + _dot(a3, b, dims))


def _rms(x, g, eps):
    return x * lax.rsqrt(jnp.mean(x * x, axis=-1, keepdims=True) + eps) * g


def _full(shape):
    return pl.BlockSpec(shape, lambda *_: (0,) * len(shape))


def _params(sem):
    return pltpu.CompilerParams(dimension_semantics=sem, vmem_limit_bytes=VMEM_LIMIT)


def _t5_bucket(n):
    nf = jnp.maximum(n, 1).astype(F32)
    large = MAX_EXACT + (jnp.log(nf / MAX_EXACT) / math.log(MAX_DISTANCE / MAX_EXACT)
                         * (N_BUCKETS - MAX_EXACT)).astype(I32)
    large = jnp.minimum(large, N_BUCKETS - 1)
    return jnp.where(n < MAX_EXACT, n, large)


def _bias_kernel(tab_ref, tiles_ref, srow_ref, *, blk):
    r = lax.broadcasted_iota(I32, (blk, blk), 0)
    c = lax.broadcasted_iota(I32, (blk, blk), 1)
    sr = lax.broadcasted_iota(I32, (8, LANES), 0)
    sc = lax.broadcasted_iota(I32, (8, LANES), 1)
    sdist = jnp.where(sr == 0, PAGE_SIZE - sc, 0)
    sbucket = _t5_bucket(sdist)
    for h in range(ATT_HEADS):
        far = tab_ref[N_BUCKETS - 1, h]
        for t in range(2):
            dist = r - c + t * blk
            bucket = _t5_bucket(jnp.maximum(dist, 0))
            val = jnp.zeros((blk, blk), F32)
            for b in range(N_BUCKETS - 1):
                val = jnp.where(bucket == b, tab_ref[b, h] - far, val)
            if t == 0:
                val = jnp.where(dist >= 0, val, NEG_INF)
            tiles_ref[h, t] = val
        sval = jnp.zeros((8, LANES), F32)
        for b in range(N_BUCKETS - 1):
            sval = jnp.where(sbucket == b, tab_ref[b, h] - far, sval)
        srow_ref[h] = jnp.where(sr < 2, sval, 0.0)


def _bias_tiles(rel_bias, blk):
    return pl.pallas_call(
        functools.partial(_bias_kernel, blk=blk),
        out_shape=(jax.ShapeDtypeStruct((ATT_HEADS, 2, blk, blk), F32),
                   jax.ShapeDtypeStruct((ATT_HEADS, 8, LANES), F32)),
        in_specs=[pl.BlockSpec(memory_space=pltpu.SMEM)],
        name="t5_bias_tiles",
    )(rel_bias)


def _proj_kernel(x_ref, g_ref, w_ref, zr_ref, q_ref, k_ref, v_ref, kb_ref, vb_ref, gate_ref):
    h = _rms(x_ref[...], g_ref[...], NORM_EPS).astype(BF16)
    o = 0
    zr_ref[...] = _dot(h, w_ref[:, o:o + RWKV_PROJ]); o += RWKV_PROJ
    q_ref[...] = _dot(h, w_ref[:, o:o + ATT_QK_W]); o += ATT_QK_W
    k = _dot(h, w_ref[:, o:o + ATT_QK_W]); o += ATT_QK_W
    v = _dot(h, w_ref[:, o:o + ATT_V_W]); o += ATT_V_W
    k_ref[...] = k
    v_ref[...] = v
    kb_ref[...] = k.astype(BF16)
    vb_ref[...] = v.astype(BF16)
    gate_ref[...] = jax.nn.sigmoid(_dot(h, w_ref[:, o:o + 2 * D_MODEL]))


def _proj(x, g, w_bf16):
    t = x.shape[0]
    tm = min(256, t)
    n_total = w_bf16.shape[1]
    row = lambda n: pl.BlockSpec((tm, n), lambda i: (i, 0))
    sds = lambda n, dt=F32: jax.ShapeDtypeStruct((t, n), dt)
    return pl.pallas_call(
        _proj_kernel,
        grid=(t // tm,),
        in_specs=[row(D_MODEL), _full((1, D_MODEL)), _full((D_MODEL, n_total))],
        out_specs=(row(RWKV_PROJ), row(ATT_QK_W), row(ATT_QK_W), row(ATT_V_W),
                   row(ATT_QK_W), row(ATT_V_W), row(2 * D_MODEL)),
        out_shape=(sds(RWKV_PROJ), sds(ATT_QK_W), sds(ATT_QK_W), sds(ATT_V_W),
                   sds(ATT_QK_W, BF16), sds(ATT_V_W, BF16), sds(2 * D_MODEL)),
        compiler_params=_params(("parallel",)),
        name="in_proj",
    )(x, g, w_bf16)


def _rwkv_pre_kernel(*refs, per_token_prev):
    if per_token_prev:
        z_ref, zp_ref = refs[:2]
        rest = refs[2:]
    else:
        z_ref, halo_ref, s0_ref = refs[:3]
        rest = refs[3:]
    (mu_ref, w0_ref, a0_ref, kk_ref, ka_ref, w2_ref, a2_ref, g2_ref, seg_ref,
     r_out, lw_out, kx_out, v_out, kk_out, kka_out, g_out) = rest
    z = z_ref[...]
    if per_token_prev:
        zp = zp_ref[...]
    else:
        first = jnp.where(pl.program_id(0) == 0, s0_ref[...], halo_ref[7:8, :])
        rowid = lax.broadcasted_iota(I32, z.shape, 0)
        zp = jnp.where(rowid == 0, first, pltpu.roll(z, 1, axis=0))
    zs = z + (zp - z) * mu_ref[...]
    r = zs[:, 0:RWKV_W]
    k = zs[:, RWKV_W:2 * RWKV_W]
    v = zs[:, 2 * RWKV_W:3 * RWKV_W]
    wa = zs[:, 3 * RWKV_W:3 * RWKV_W + DECAY_LORA + AAA_LORA]
    g_lo = zs[:, 3 * RWKV_W + DECAY_LORA + AAA_LORA:]
    wpre = w0_ref[...] + _mm(jnp.tanh(wa), w2_ref[...], passes=3)
    nw = -wpre
    softplus = jnp.maximum(nw, 0.0) + jnp.log1p(jnp.exp(-jnp.abs(nw)))
    w = -softplus - 0.5
    a = jax.nn.sigmoid(a0_ref[...] + _mm(wa, a2_ref[...], passes=3))
    g = _mm(jax.nn.sigmoid(g_lo), g2_ref[...], passes=3)
    kk = k * kk_ref[...]
    nrm = jnp.sqrt(_mm_exact(kk * kk, seg_ref[...]))
    kk = kk / jnp.maximum(nrm, 1e-12)
    r_out[...] = r
    lw_out[...] = -jnp.exp(w)
    kx_out[...] = k * (1.0 + (a - 1.0) * ka_ref[...])
    v_out[...] = v
    kk_out[...] = kk
    kka_out[...] = kk * a
    g_out[...] = g


def _rwkv_pre(z, zprev_or_shift0, per_token_prev, wts):
    t = z.shape[0]
    tm = min(256, t)
    row = lambda n: pl.BlockSpec((tm, n), lambda i: (i, 0))
    vec = lambda n: _full((1, n))
    if per_token_prev:
        lead = [row(RWKV_PROJ), row(RWKV_PROJ)]
        args = [z, zprev_or_shift0]
    else:
        halo = pl.BlockSpec((8, RWKV_PROJ), lambda i: (jnp.maximum(i * (tm // 8) - 1, 0), 0))
        lead = [row(RWKV_PROJ), halo, vec(RWKV_PROJ)]
        args = [z, z, zprev_or_shift0]
    lora_w = DECAY_LORA + AAA_LORA
    return pl.pallas_call(
        functools.partial(_rwkv_pre_kernel, per_token_prev=per_token_prev),
        grid=(t // tm,),
        in_specs=lead + [vec(RWKV_PROJ), vec(RWKV_W), vec(RWKV_W), vec(RWKV_W), vec(RWKV_W),
                         _full((lora_w, RWKV_W)), _full((lora_w, RWKV_W)), _full((GATE_LORA, RWKV_W)),
                         _full((RWKV_W, RWKV_W))],
        out_specs=tuple(row(RWKV_W) for _ in range(7)),
        out_shape=tuple(jax.ShapeDtypeStruct((t, RWKV_W), F32) for _ in range(7)),
        compiler_params=_params(("parallel",)),
        name="rwkv_pre",
    )(*args, wts['mu'], wts['w0'], wts['a0'], wts['k_k'], wts['k_a'], wts['w2p'], wts['a2p'], wts['g2'],
      wts['seg64'])


def _stack(x, lane_head):
    return jnp.concatenate([jnp.where(lane_head == h, x, 0.0) for h in range(4)], axis=0)


def _rwkv_chunk_quad(r, lw, kx, v, kk, ka, s, tri, passes):
    L = CHUNK
    lane_head = lax.broadcasted_iota(I32, (L, QUAD), 1) // RWKV_HEAD
    cs = _mm_exact(tri, lw)
    cs_end = cs[L - 1:L, :]
    e_neg = jnp.exp(-cs)
    at = -kk * jnp.exp(cs - lw)
    bt = ka * e_neg
    kt = kx * e_neg
    rt = r * jnp.exp(cs)
    e_end = jnp.exp(cs_end - cs)
    bh = ka * e_end
    kh = kx * e_end
    s_a, s_b, s_k, s_r = (_stack(x, lane_head) for x in (at, bt, kt, rt))
    s_v, s_bh, s_kh = (_stack(x, lane_head) for x in (v, bh, kh))
    rr = lax.broadcasted_iota(I32, (4 * L, 4 * L), 0) % L
    cc = lax.broadcasted_iota(I32, (4 * L, 4 * L), 1) % L
    strict = rr > cc
    incl = rr >= cc
    mm = functools.partial(_mm, passes=passes)
    n_ab = jnp.where(strict, mm(s_a, s_b, NT), 0.0)
    n_ak = jnp.where(strict, mm(s_a, s_k, NT), 0.0)
    n_rb = jnp.where(incl, mm(s_r, s_b, NT), 0.0)
    n_rk = jnp.where(incl, mm(s_r, s_k, NT), 0.0)
    eye = (lax.broadcasted_iota(I32, (4 * L, 4 * L), 0) == lax.broadcasted_iota(I32, (4 * L, 4 * L), 1))
    x = n_ab
    tinv = jnp.where(eye, 1.0, 0.0) + x
    for _ in range(5):
        x = mm(x, x)
        tinv = tinv + mm(tinv, x)
    q = mm(s_a, s, NT) + mm(n_ak, s_v)
    s_u = mm(tinv, q)
    s_y = mm(s_r, s, NT) + mm(n_rb, s_u) + mm(n_rk, s_v)
    y = jnp.zeros((L, QUAD), F32)
    for h in range(4):
        y = y + jnp.where(lane_head == h, s_y[h * L:(h + 1) * L, :], 0.0)
    s_new = s * jnp.exp(cs_end) + mm(s_u, s_bh, TN) + mm(s_v, s_kh, TN)
    return y, s_new


def _rwkv_chunk_kernel(r_ref, lw_ref, kx_ref, v_ref, kk_ref, ka_ref, tri_ref, y_ref, sout_ref, s_scr,
                       *, nsub, passes):
    @pl.when(pl.program_id(0) == 0)
    def _():
        s_scr[...] = jnp.zeros_like(s_scr)

    tri = tri_ref[...]
    for c in range(nsub):
        rows = slice(c * CHUNK, (c + 1) * CHUNK)
        for qd in range(RWKV_W // QUAD):
            cols = slice(qd * QUAD, (qd + 1) * QUAD)
            y, s_new = _rwkv_chunk_quad(r_ref[rows, cols], lw_ref[rows, cols], kx_ref[rows, cols],
                                        v_ref[rows, cols], kk_ref[rows, cols], ka_ref[rows, cols],
                                        s_scr[qd], tri, passes)
            y_ref[rows, cols] = y
            s_scr[qd] = s_new

    @pl.when(pl.program_id(0) == pl.num_programs(0) - 1)
    def _():
        sout_ref[...] = s_scr[...]


def _rwkv_chunked(r, lw, kx, v, kk, ka, passes=3):
    t = r.shape[0]
    nsub = 2 if t % (2 * CHUNK) == 0 else 1
    lb = nsub * CHUNK
    nq = RWKV_W // QUAD
    tri = (jnp.arange(CHUNK)[:, None] >= jnp.arange(CHUNK)[None, :]).astype(BF16)
    row = pl.BlockSpec((lb, RWKV_W), lambda i: (i, 0))
    y, s_bd = pl.pallas_call(
        functools.partial(_rwkv_chunk_kernel, nsub=nsub, passes=passes),
        grid=(t // lb,),
        in_specs=[row] * 6 + [_full((CHUNK, CHUNK))],
        out_specs=(row, _full((nq, QUAD, QUAD))),
        out_shape=(jax.ShapeDtypeStruct((t, RWKV_W), F32), jax.ShapeDtypeStruct((nq, QUAD, QUAD), F32)),
        scratch_shapes=[pltpu.VMEM((nq, QUAD, QUAD), F32)],
        compiler_params=_params(("arbitrary",)),
        name="rwkv_chunked",
    )(r, lw, kx, v, kk, ka, tri)
    blocks = [s_bd[h // 4, (h % 4) * RWKV_HEAD:(h % 4 + 1) * RWKV_HEAD, (h % 4) * RWKV_HEAD:(h % 4 + 1) * RWKV_HEAD]
              for h in range(RWKV_HEADS)]
    return y, jnp.stack(blocks)


def _rwkv_step_kernel(s_ref, r_ref, lw_ref, kx_ref, kk_ref, ka_ref, v_ref, sout_ref, y_ref):
    s = s_ref[...]
    sa = -jnp.sum(s * kk_ref[...], axis=-1, keepdims=True)
    s_new = s * jnp.exp(lw_ref[...]) + sa * ka_ref[...] + v_ref[...] * kx_ref[...]
    sout_ref[...] = s_new
    y_ref[...] = jnp.sum(s_new * r_ref[...], axis=-1, keepdims=True)


def _rwkv_step(state, r, lw, kx, kk, ka, v):
    db = state.shape[0]
    bb = 8 if db % 8 == 0 else 1
    rowv = lambda a: a.reshape(db, RWKV_HEADS, 1, RWKV_HEAD)
    st = pl.BlockSpec((bb, RWKV_HEADS, RWKV_HEAD, RWKV_HEAD), lambda i: (i, 0, 0, 0))
    rv = pl.BlockSpec((bb, RWKV_HEADS, 1, RWKV_HEAD), lambda i: (i, 0, 0, 0))
    cv = pl.BlockSpec((bb, RWKV_HEADS, RWKV_HEAD, 1), lambda i: (i, 0, 0, 0))
    s_new, y = pl.pallas_call(
        _rwkv_step_kernel,
        grid=(db // bb,),
        in_specs=[st, rv, rv, rv, rv, rv, cv],
        out_specs=(st, cv),
        out_shape=(jax.ShapeDtypeStruct(state.shape, F32),
                   jax.ShapeDtypeStruct((db, RWKV_HEADS, RWKV_HEAD, 1), F32)),
        compiler_params=_params(("parallel",)),
        name="rwkv_step",
    )(state, rowv(r), rowv(lw), rowv(kx), rowv(kk), rowv(ka), v.reshape(db, RWKV_HEADS, RWKV_HEAD, 1))
    return y.reshape(db, RWKV_W), s_new


def _attn_prompt_kernel(lam_ref, q_ref, k_ref, v_ref, bias_ref, o_ref, m_ref, l_ref, acc_ref, *, blk):
    i = pl.program_id(1)
    lane = lax.broadcasted_iota(I32, (blk, LANES), 1)
    qs = q_ref[...] * (HEAD_QK ** -0.5)
    qmaps = (jnp.where(lane < HEAD_QK, qs, 0.0).astype(BF16), jnp.where(lane >= HEAD_QK, qs, 0.0).astype(BF16))
    m_ref[...] = jnp.full_like(m_ref, -jnp.inf)
    l_ref[...] = jnp.zeros_like(l_ref)
    acc_ref[...] = jnp.zeros_like(acc_ref)

    def update(j, bias):
        start = pl.multiple_of(j * blk, blk)
        kb = k_ref[pl.ds(start, blk), :]
        vb = v_ref[pl.ds(start, blk), :]
        for mi in range(2):
            s = _dot(qmaps[mi], kb, NT)
            if bias is not None:
                s = s + bias
            m_old = m_ref[mi]
            m_new = jnp.maximum(m_old, jnp.max(s, axis=-1, keepdims=True))
            alpha = jnp.exp(m_old - m_new)
            p = jnp.exp(s - m_new)
            l_ref[mi] = alpha * l_ref[mi] + jnp.sum(p, axis=-1, keepdims=True)
            acc_ref[mi] = alpha * acc_ref[mi] + _dot(p.astype(BF16), vb)
            m_ref[mi] = m_new

    def far_body(j, carry):
        update(j, None)
        return carry

    lax.fori_loop(0, jnp.maximum(i - 1, 0), far_body, 0)

    @pl.when(i >= 1)
    def _():
        update(i - 1, bias_ref[1])

    update(i, bias_ref[0])
    o_ref[...] = acc_ref[0] / l_ref[0] - lam_ref[0] * (acc_ref[1] / l_ref[1])


def _attn_prompt(lam, q, kb, vb, tiles, blk):
    t = q.shape[0]
    return pl.pallas_call(
        functools.partial(_attn_prompt_kernel, blk=blk),
        grid=(ATT_HEADS, t // blk),
        in_specs=[pl.BlockSpec(memory_space=pltpu.SMEM),
                  pl.BlockSpec((blk, LANES), lambda h, i: (i, h)),
                  pl.BlockSpec((t, LANES), lambda h, i: (0, h)),
                  pl.BlockSpec((t, LANES), lambda h, i: (0, h)),
                  pl.BlockSpec((None, 2, blk, blk), lambda h, i: (h, 0, 0, 0))],
        out_specs=pl.BlockSpec((blk, LANES), lambda h, i: (i, h)),
        out_shape=jax.ShapeDtypeStruct((t, ATT_V_W), F32),
        scratch_shapes=[pltpu.VMEM((2, blk, 1), F32), pltpu.VMEM((2, blk, 1), F32),
                        pltpu.VMEM((2, blk, LANES), F32)],
        compiler_params=_params(("parallel", "parallel")),
        name="attn_prompt",
    )(lam, q, kb, vb, tiles)


def _attn_sample_kernel(pt_ref, lam_ref, q_ref, kn_ref, vn_ref, srow_ref, *rest, pages_per_step):
    del pt_ref
    pp = pages_per_step
    k_refs = rest[:pp]
    v_refs = rest[pp:2 * pp]
    o_ref, m_ref, l_ref, acc_ref = rest[2 * pp:]
    g = pl.program_id(1)
    last = g == pl.num_programs(1) - 1
    row = lax.broadcasted_iota(I32, (8, LANES), 0)
    lane = lax.broadcasted_iota(I32, (8, LANES), 1)

    def q8(h):
        qh = q_ref[0, h:h + 1, :] * (HEAD_QK ** -0.5)
        return jnp.where(((row == 0) & (lane < HEAD_QK)) | ((row == 1) & (lane >= HEAD_QK)), qh, 0.0)

    @pl.when(g == 0)
    def _():
        for h in range(ATT_HEADS):
            s_self = jnp.sum(q8(h) * kn_ref[0, h:h + 1, :], axis=-1, keepdims=True) + srow_ref[h, 1:2, 0:1]
            m_ref[h] = s_self
            l_ref[h] = jnp.ones((8, 1), F32)
            acc_ref[h] = jnp.broadcast_to(vn_ref[0, h:h + 1, :], (8, LANES))

    for h in range(ATT_HEADS):
        qh = q8(h).astype(BF16)
        ss = []
        for p in range(pp):
            s = _dot(qh, k_refs[p][:, h, :].astype(BF16), NT)
            if p == pp - 1:
                s = s + jnp.where(last, srow_ref[h, 0:1, :], 0.0)
            ss.append(s)
        s_all = jnp.concatenate(ss, axis=-1)
        m_old = m_ref[h]
        m_new = jnp.maximum(m_old, jnp.max(s_all, axis=-1, keepdims=True))
        alpha = jnp.exp(m_old - m_new)
        p_all = jnp.exp(s_all - m_new)
        l_ref[h] = alpha * l_ref[h] + jnp.sum(p_all, axis=-1, keepdims=True)
        acc = alpha * acc_ref[h]
        for p in range(pp):
            acc = acc + _dot(p_all[:, p * PAGE_SIZE:(p + 1) * PAGE_SIZE].astype(BF16),
                             v_refs[p][:, h, :].astype(BF16))
        acc_ref[h] = acc
        m_ref[h] = m_new

    @pl.when(last)
    def _():
        for h in range(ATT_HEADS):
            o = acc_ref[h] / l_ref[h]
            o_ref[0, h:h + 1, :] = o[0:1, :] - lam_ref[0] * o[1:2, :]


def _attn_sample(lam, q, k_new, v_new, cache_k, cache_v, layer, page_table, srow):
    db, n_pages = page_table.shape
    pp = 8 if n_pages % 8 == 0 else 1
    tok = pl.BlockSpec((1, ATT_HEADS, LANES), lambda b, g, pt: (b, 0, 0))

    def page_spec(p):
        return pl.BlockSpec((None, None, PAGE_SIZE, ATT_HEADS, LANES),
                            lambda b, g, pt: (layer, pt[b, g * pp + p], 0, 0, 0))

    shp = (db, ATT_HEADS, LANES)
    out = pl.pallas_call(
        functools.partial(_attn_sample_kernel, pages_per_step=pp),
        grid_spec=pltpu.PrefetchScalarGridSpec(
            num_scalar_prefetch=1,
            grid=(db, n_pages // pp),
            in_specs=[pl.BlockSpec(memory_space=pltpu.SMEM), tok, tok, tok,
                      pl.BlockSpec((ATT_HEADS, 8, LANES), lambda b, g, pt: (0, 0, 0))]
                     + [page_spec(p) for p in range(pp)] * 2,
            out_specs=tok,
            scratch_shapes=[pltpu.VMEM((ATT_HEADS, 8, 1), F32), pltpu.VMEM((ATT_HEADS, 8, 1), F32),
                            pltpu.VMEM((ATT_HEADS, 8, LANES), F32)]),
        out_shape=jax.ShapeDtypeStruct(shp, F32),
        compiler_params=_params(("parallel", "arbitrary")),
        name="attn_sample",
    )(page_table, lam, q.reshape(shp), k_new.reshape(shp), v_new.reshape(shp), srow,
      *([cache_k] * pp), *([cache_v] * pp))
    return out.reshape(db, ATT_V_W)


def _merge_kernel(x_ref, y_ref, r_ref, kx_ref, v_ref, g_ref, o_ref, gate_ref,
                  lnw_ref, lnb_ref, rk_ref, sub_ref, seg_ref, wbr_ref, wba_ref, wout_ref, nffn_ref,
                  x1_ref, xnt_ref, *, attn_scale):
    seg = seg_ref[...]
    y = y_ref[...]
    inv_n = 1.0 / RWKV_HEAD
    mean = _mm_exact(y, seg) * inv_n
    yc = y - mean
    var = _mm_exact(yc * yc, seg) * inv_n
    yn = yc * lax.rsqrt(var + RWKV_LN_EPS) * lnw_ref[...] + lnb_ref[...]
    v = v_ref[...]
    bonus = _mm_exact(r_ref[...] * kx_ref[...] * rk_ref[...], seg) * v
    y_rwkv = (yn + bonus) * g_ref[...]
    o = o_ref[...]
    parts = []
    for h in range(ATT_HEADS):
        oh = o[:, h * HEAD_V:(h + 1) * HEAD_V]
        parts.append(_rms(oh, sub_ref[...], SUBLN_EPS) * attn_scale)
    y_attn = jnp.concatenate(parts, axis=-1)
    gates = gate_ref[...]
    merged = (gates[:, :D_MODEL] * _mm(y_rwkv, wbr_ref[...])
              + gates[:, D_MODEL:] * _mm(y_attn, wba_ref[...]))
    x1 = x_ref[...] + _mm(merged, wout_ref[...])
    x1_ref[...] = x1
    xnt_ref[...] = _rms(x1, nffn_ref[...], NORM_EPS).T


def _merge(x, y, r, kx, v, g, o, gates, wts, attn_scale):
    t = x.shape[0]
    tm = min(256, t)
    row = lambda n: pl.BlockSpec((tm, n), lambda i: (i, 0))
    vec = lambda n: _full((1, n))
    return pl.pallas_call(
        functools.partial(_merge_kernel, attn_scale=attn_scale),
        grid=(t // tm,),
        in_specs=[row(D_MODEL)] + [row(RWKV_W)] * 6 + [row(2 * D_MODEL),
                  vec(RWKV_W), vec(RWKV_W), vec(RWKV_W), vec(HEAD_V), _full((RWKV_W, RWKV_W)),
                  _full((RWKV_W, D_MODEL)), _full((ATT_V_W, D_MODEL)), _full((D_MODEL, D_MODEL)),
                  vec(D_MODEL)],
        out_specs=(row(D_MODEL), pl.BlockSpec((D_MODEL, tm), lambda i: (0, i))),
        out_shape=(jax.ShapeDtypeStruct((t, D_MODEL), F32), jax.ShapeDtypeStruct((D_MODEL, t), F32)),
        compiler_params=_params(("parallel",)),
        name="merge",
    )(x, y, r, kx, v, g, o, gates, wts['ln_w'], wts['ln_b'], wts['r_k'], wts['subln'], wts['seg64'],
      wts['w_br_rwkv'], wts['w_br_attn'], wts['w_out'], wts['norm_ffn'])


def _top_sorted(vals, k):
    rows = []
    for _ in range(k):
        mx = jnp.max(vals, axis=0, keepdims=True)
        rows.append(mx)
        vals = jnp.where(vals == mx, -jnp.inf, vals)
    return rows


def _peer_route_kernel(xnt_ref, wqh_ref, wql_ref, k1_ref, k2_ref,
                       s1_ref, s2_ref, e1_ref, e2_ref, tau_ref):
    xh, xl = _split(xnt_ref[...], 2)
    wqh = wqh_ref[...]
    qt = _dot(wqh, xh) + (_dot(wqh, xl) + _dot(wql_ref[...], xh))
    for h in range(PEER_HEADS):
        base = h * PEER_QDIM
        s1 = _mm(k1_ref[...], qt[base:base + PEER_HALF, :], passes=3)
        s2 = _mm(k2_ref[...], qt[base + PEER_HALF:base + PEER_QDIM, :], passes=3)
        v1 = _top_sorted(s1, PEER_TOPK)
        v2 = _top_sorted(s2, PEER_TOPK)
        v2m = jnp.concatenate(v2, axis=0)
        cand = jnp.concatenate([v1[i] + v2m for i in range(PEER_TOPK)], axis=0)
        tau = _top_sorted(cand, PEER_TOPK)[-1]
        top = v1[0] + v2[0]
        z = jnp.sum(jnp.where(cand >= tau, jnp.exp(cand - top), 0.0), axis=0, keepdims=True)
        s1_ref[h] = s1
        s2_ref[h] = s2
        e1_ref[h] = jnp.where(s1 >= v1[-1], jnp.exp(s1 - v1[0]), 0.0) / z
        e2_ref[h] = jnp.where(s2 >= v2[-1], jnp.exp(s2 - v2[0]), 0.0)
        tau_ref[h] = tau


def _peer_route(xnt, wts):
    t = xnt.shape[1]
    tb = min(256, t)
    col = pl.BlockSpec((PEER_HEADS, N_KEYS, tb), lambda i: (0, 0, i))
    big = jax.ShapeDtypeStruct((PEER_HEADS, N_KEYS, t), F32)
    return pl.pallas_call(
        _peer_route_kernel,
        grid=(t // tb,),
        in_specs=[pl.BlockSpec((D_MODEL, tb), lambda i: (0, i)),
                  _full((PEER_HEADS * PEER_QDIM, D_MODEL)), _full((PEER_HEADS * PEER_QDIM, D_MODEL)),
                  _full((N_KEYS, PEER_HALF)), _full((N_KEYS, PEER_HALF))],
        out_specs=(col, col, col, col, pl.BlockSpec((PEER_HEADS, 1, tb), lambda i: (0, 0, i))),
        out_shape=(big, big, big, big, jax.ShapeDtypeStruct((PEER_HEADS, 1, t), F32)),
        compiler_params=_params(("parallel",)),
        name="peer_route",
    )(xnt, wts['wqT_hi'], wts['wqT_lo'], wts['peer_k1'], wts['peer_k2'])


def _peer_dense_kernel(xnt_ref, s1_ref, s2_ref, e1_ref, e2_ref, tau_ref, eu_ref, evt_ref, x1_ref,
                       o_ref, yt_ref, *, eb):
    j = pl.program_id(1)

    @pl.when(j == 0)
    def _():
        yt_ref[...] = jnp.zeros_like(yt_ref)

    u = _dot(eu_ref[...], xnt_ref[...].astype(BF16))
    act = 0.5 * u * (1.0 + lax.erf(u * (2.0 ** -0.5)))
    groups = eb // N_KEYS
    parts = []
    for a in range(groups):
        i1 = j * groups + a
        w = None
        for h in range(PEER_HEADS):
            a1 = s1_ref[h, pl.ds(i1, 1), :]
            g1 = e1_ref[h, pl.ds(i1, 1), :]
            wh = jnp.where(s2_ref[h] + a1 >= tau_ref[h], e2_ref[h], 0.0) * g1
            w = wh if w is None else w + wh
        parts.append(w * act[a * N_KEYS:(a + 1) * N_KEYS, :])
    gated = jnp.concatenate(parts, axis=0).astype(BF16)
    yt_ref[...] += _dot(evt_ref[...], gated)

    @pl.when(j == pl.num_programs(1) - 1)
    def _():
        o_ref[...] = x1_ref[...] + yt_ref[...].T


def _peer_dense(xnt, route, x1, wts):
    t = x1.shape[0]
    tb = min(256, t)
    eb = 512
    s1, s2, e1, e2, tau = route
    col = pl.BlockSpec((PEER_HEADS, N_KEYS, tb), lambda i, j: (0, 0, i))
    return pl.pallas_call(
        functools.partial(_peer_dense_kernel, eb=eb),
        grid=(t // tb, N_EXPERTS // eb),
        in_specs=[pl.BlockSpec((D_MODEL, tb), lambda i, j: (0, i)), col, col, col, col,
                  pl.BlockSpec((PEER_HEADS, 1, tb), lambda i, j: (0, 0, i)),
                  pl.BlockSpec((eb, D_MODEL), lambda i, j: (j, 0)),
                  pl.BlockSpec((D_MODEL, eb), lambda i, j: (0, j)),
                  pl.BlockSpec((tb, D_MODEL), lambda i, j: (i, 0))],
        out_specs=pl.BlockSpec((tb, D_MODEL), lambda i, j: (i, 0)),
        out_shape=jax.ShapeDtypeStruct((t, D_MODEL), F32),
        scratch_shapes=[pltpu.VMEM((D_MODEL, tb), F32)],
        compiler_params=_params(("parallel", "arbitrary")),
        name="peer_dense",
    )(xnt, s1, s2, e1, e2, tau, wts['eu'], wts['evT'], x1)


def _ple_kernel(x_ref, pe_ref, wple_ref, nple_ref, wgate_ref, nfin_ref, o_ref, *, final_norm):
    x = x_ref[...]
    gate = jax.nn.sigmoid(_mm(_rms(x, nple_ref[...], NORM_EPS), wgate_ref[...]))
    x = x + _mm(pe_ref[...], wple_ref[...]) * gate
    o_ref[...] = _rms(x, nfin_ref[...], NORM_EPS) if final_norm else x


def _ple(x, pe, wts, final_norm):
    t = x.shape[0]
    tm = min(256, t)
    row = lambda n: pl.BlockSpec((tm, n), lambda i: (i, 0))
    return pl.pallas_call(
        functools.partial(_ple_kernel, final_norm=final_norm),
        grid=(t // tm,),
        in_specs=[row(D_MODEL), row(PLE_DIM), _full((PLE_DIM, D_MODEL)), _full((1, D_MODEL)),
                  _full((D_MODEL, D_MODEL)), _full((1, D_MODEL))],
        out_specs=row(D_MODEL),
        out_shape=jax.ShapeDtypeStruct((t, D_MODEL), F32),
        compiler_params=_params(("parallel",)),
        name="ple_final",
    )(x, pe, wts['w_ple'], wts['norm_ple'], wts['w_ple_gate'], wts['norm_final'])


def _token_stages(x, pe, y_rwkv_raw, pre, o_attn, gates, wts, attn_scale, final_norm):
    r, _, kx, v, _, _, g = pre
    x1, xnt = _merge(x, y_rwkv_raw, r, kx, v, g, o_attn, gates, wts, attn_scale)
    x2 = _peer_dense(xnt, _peer_route(xnt, wts), x1, wts)
    return _ple(x2, pe, wts, final_norm)


def kernel(x_prompt, x_sample, cache_k, cache_v, state_wkv, state_shift, page_table, p_prompt, p_sample,
           norm_mix, w_in, rwkv_mu, rwkv_w0, rwkv_w2, rwkv_a0, rwkv_a2, rwkv_g2, rwkv_k_k, rwkv_k_a,
           rwkv_r_k, rwkv_ln_w, rwkv_ln_b, attn_lq1, attn_lk1, attn_lq2, attn_lk2, attn_subln, rel_bias,
           w_br_rwkv, w_br_attn, w_out, norm_ffn, peer_wq, peer_k1, peer_k2, peer_u, peer_v,
           norm_ple, w_ple, w_ple_gate, norm_final):
    depth = w_in.shape[0]
    bsz, seq = x_prompt.shape[0], x_prompt.shape[1]
    db, dseq = x_sample.shape[0], x_sample.shape[1]
    assert bsz == 1 and dseq == 1, "prompt batch and decode length are fixed at 1"
    blk = 256 if seq % 256 == 0 else seq
    xp = x_prompt.reshape(seq, D_MODEL)
    xs = x_sample.reshape(db, D_MODEL)
    tiles, srow = _bias_tiles(rel_bias, blk)
    seg64 = (jnp.arange(RWKV_W)[:, None] // RWKV_HEAD == jnp.arange(RWKV_W)[None, :] // RWKV_HEAD).astype(BF16)
    row = lambda a: a.reshape(1, -1)
    outs = [[] for _ in range(8)]
    for l in range(depth):
        zpad = jnp.zeros((DECAY_LORA, RWKV_W), F32)
        wq_t = peer_wq[l].T
        wq_hi = wq_t.astype(BF16)
        wts = {
            'mu': row(rwkv_mu[l]), 'w0': row(rwkv_w0[l]), 'a0': row(rwkv_a0[l]),
            'k_k': row(rwkv_k_k[l]), 'k_a': row(rwkv_k_a[l]),
            'w2p': jnp.concatenate([rwkv_w2[l], zpad], axis=0),
            'a2p': jnp.concatenate([zpad, rwkv_a2[l]], axis=0),
            'g2': rwkv_g2[l], 'seg64': seg64,
            'ln_w': row(rwkv_ln_w[l]), 'ln_b': row(rwkv_ln_b[l]), 'r_k': row(rwkv_r_k[l]),
            'subln': row(attn_subln[l]),
            'w_br_rwkv': w_br_rwkv[l].astype(BF16), 'w_br_attn': w_br_attn[l].astype(BF16),
            'w_out': w_out[l].astype(BF16), 'norm_ffn': row(norm_ffn[l]),
            'wqT_hi': wq_hi, 'wqT_lo': (wq_t - wq_hi.astype(F32)).astype(BF16),
            'peer_k1': peer_k1[l], 'peer_k2': peer_k2[l],
            'eu': peer_u[l].astype(BF16), 'evT': peer_v[l].T.astype(BF16),
            'w_ple': w_ple[l].astype(BF16), 'norm_ple': row(norm_ple[l]),
            'w_ple_gate': w_ple_gate[l].astype(BF16), 'norm_final': row(norm_final),
        }
        lam_init = 0.8 - 0.6 * math.exp(-0.3 * l)
        lam = (jnp.exp(jnp.sum(attn_lq1[l] * attn_lk1[l])) - jnp.exp(jnp.sum(attn_lq2[l] * attn_lk2[l]))
               + lam_init).reshape(1).astype(F32)
        w_in_b = w_in[l].astype(BF16)
        g_mix = row(norm_mix[l])

        zr, q, k, v, kb, vb, gates = _proj(xp, g_mix, w_in_b)
        pre = _rwkv_pre(zr, jnp.zeros((1, RWKV_PROJ), F32), False, wts)
        y_raw, wkv_p = _rwkv_chunked(*pre[:6])
        o = _attn_prompt(lam, q, kb, vb, tiles, blk)
        xp = _token_stages(xp, p_prompt[l].reshape(seq, PLE_DIM), y_raw, pre, o, gates, wts,
                           1.0 - lam_init, l == depth - 1)
        outs[0].append(k.reshape(1, seq, ATT_HEADS, 2 * HEAD_QK))
        outs[1].append(v.reshape(1, seq, ATT_HEADS, HEAD_V))
        outs[4].append(wkv_p[None])
        outs[6].append(zr[seq - 1:seq, :])

        zr, q, k, v, kb, vb, gates = _proj(xs, g_mix, w_in_b)
        pre = _rwkv_pre(zr, state_shift[l], True, wts)
        r, lw, kx, vv, kk, ka, _ = pre
        y_raw, wkv_s = _rwkv_step(state_wkv[l], r, lw, kx, kk, ka, vv)
        o = _attn_sample(lam, q, k, v, cache_k, cache_v, l, page_table, srow)
        xs = _token_stages(xs, p_sample[l].reshape(db, PLE_DIM), y_raw, pre, o, gates, wts,
                           1.0 - lam_init, l == depth - 1)
        outs[2].append(k.reshape(db, 1, ATT_HEADS, 2 * HEAD_QK))
        outs[3].append(v.reshape(db, 1, ATT_HEADS, HEAD_V))
        outs[5].append(wkv_s)
        outs[7].append(zr)
    y_prompt = xp.reshape(1, seq, D_MODEL)
    y_sample = xs.reshape(db, 1, D_MODEL)
    st = lambda i: jnp.stack(outs[i])
    return (y_prompt, y_sample, st(0), st(1), st(2), st(3), st(4), st(5), st(6), st(7))
```

```python
import functools
import math

import jax
import jax.numpy as jnp
from jax import lax
from jax.experimental import pallas as pl
from jax.experimental.pallas import tpu as pltpu

F32 = jnp.float32
BF16 = jnp.bfloat16
I32 = jnp.int32

D_MODEL = 1024
PAGE_SIZE = 128
RWKV_HEADS = 8
RWKV_HEAD = 64
RWKV_W = RWKV_HEADS * RWKV_HEAD
DECAY_LORA = 64
AAA_LORA = 64
GATE_LORA = 128
RWKV_PROJ = 3 * RWKV_W + DECAY_LORA + AAA_LORA + GATE_LORA
RWKV_LN_EPS = 64e-5
ATT_HEADS = 4
HEAD_QK = 64
HEAD_V = 2 * HEAD_QK
ATT_QK_W = ATT_HEADS * 2 * HEAD_QK
ATT_V_W = ATT_HEADS * HEAD_V
SUBLN_EPS = 1e-5
N_BUCKETS = 32
MAX_EXACT = N_BUCKETS // 2
MAX_DISTANCE = 128
PEER_HEADS = 8
PEER_QDIM = 256
PEER_HALF = PEER_QDIM // 2
N_KEYS = 128
N_EXPERTS = N_KEYS * N_KEYS
PEER_TOPK = 16
PLE_DIM = 256
NORM_EPS = 1e-6
NEG_INF = -1e30

LANES = 128
QUAD = 4 * RWKV_HEAD
CHUNK = 64
VMEM_LIMIT = 56 * 1024 * 1024
LOG2E = math.log2(math.e)
RWKV_PASSES = 1

NN = (((1,), (0,)), ((), ()))
NT = (((1,), (1,)), ((), ()))
TN = (((0,), (0,)), ((), ()))


def _dot(a, b, dims=NN):
    return lax.dot_general(a, b, dims, preferred_element_type=F32)


def _split(x, n):
    parts = []
    for _ in range(n):
        p = x.astype(BF16)
        parts.append(p)
        x = x - p.astype(F32)
    return parts


def _mm(a, b, dims=NN, passes=1):
    if passes == 1:
        return _dot(a.astype(BF16), b.astype(BF16), dims)
    ah, al = _split(a, 2)
    bh, bl = _split(b, 2)
    return _dot(ah, bh, dims) + (_dot(ah, bl, dims) + _dot(al, bh, dims))


def _mm_exact(a, b, dims=NN):
    a1, a2, a3 = _split(a, 3)
    return _dot(a1, b, dims) + (_dot(a2, b, dims) + _dot(a3, b, dims))


def _rms(x, g, eps):
    return x * lax.rsqrt(jnp.mean(x * x, axis=-1, keepdims=True) + eps) * g


def _full(shape):
    return pl.BlockSpec(shape, lambda *_: (0,) * len(shape))


def _params(sem):
    return pltpu.CompilerParams(dimension_semantics=sem, vmem_limit_bytes=VMEM_LIMIT)


def _t5_bucket(n):
    nf = jnp.maximum(n, 1).astype(F32)
    large = MAX_EXACT + (jnp.log(nf / MAX_EXACT) / math.log(MAX_DISTANCE / MAX_EXACT)
                         * (N_BUCKETS - MAX_EXACT)).astype(I32)
    large = jnp.minimum(large, N_BUCKETS - 1)
    return jnp.where(n < MAX_EXACT, n, large)


def _bias_kernel(tab_ref, tiles_ref, sb_ref, *, blk):
    r = lax.broadcasted_iota(I32, (blk, blk), 0)
    c = lax.broadcasted_iota(I32, (blk, blk), 1)
    for t in range(2):
        dist = c - r + t * blk
        bucket = _t5_bucket(jnp.maximum(dist, 0))
        for h in range(ATT_HEADS):
            far = tab_ref[N_BUCKETS - 1, h]
            val = jnp.zeros((blk, blk), F32)
            for b in range(N_BUCKETS - 1):
                val = jnp.where(bucket == b, (tab_ref[b, h] - far) * LOG2E, val)
            if t == 0:
                val = jnp.where(dist >= 0, val, NEG_INF)
            tiles_ref[h, t] = val
    page_w = ATT_HEADS * PAGE_SIZE
    sr = lax.broadcasted_iota(I32, (2 * ATT_HEADS, page_w + LANES), 0)
    sc = lax.broadcasted_iota(I32, (2 * ATT_HEADS, page_w + LANES), 1)
    sbucket = _t5_bucket(jnp.where(sc < page_w, PAGE_SIZE - (sc >> 2), 0))
    sval = jnp.zeros(sr.shape, F32)
    for h in range(ATT_HEADS):
        far = tab_ref[N_BUCKETS - 1, h]
        hval = jnp.zeros(sr.shape, F32)
        for b in range(N_BUCKETS - 1):
            hval = jnp.where(sbucket == b, (tab_ref[b, h] - far) * LOG2E, hval)
        sval = jnp.where((sr >> 1) == h, hval, sval)
    sb_ref[...] = sval


def _bias_tiles(rel_bias, blk):
    return pl.pallas_call(
        functools.partial(_bias_kernel, blk=blk),
        out_shape=(jax.ShapeDtypeStruct((ATT_HEADS, 2, blk, blk), F32),
                   jax.ShapeDtypeStruct((2 * ATT_HEADS, ATT_HEADS * PAGE_SIZE + LANES), F32)),
        in_specs=[pl.BlockSpec(memory_space=pltpu.SMEM)],
        name="t5_bias_tiles",
    )(rel_bias)


def _proj_kernel(x_ref, g_ref, w_ref, zr_ref, q_ref, k_ref, v_ref, kb_ref, vt_ref, gate_ref):
    h = _rms(x_ref[...], g_ref[...], NORM_EPS).astype(BF16)
    o = 0
    zr_ref[...] = _dot(h, w_ref[:, o:o + RWKV_PROJ]); o += RWKV_PROJ
    q_ref[...] = _dot(h, w_ref[:, o:o + ATT_QK_W]); o += ATT_QK_W
    k = _dot(h, w_ref[:, o:o + ATT_QK_W]); o += ATT_QK_W
    v = _dot(h, w_ref[:, o:o + ATT_V_W]); o += ATT_V_W
    k_ref[...] = k
    v_ref[...] = v
    kb_ref[...] = k.astype(BF16)
    vt_ref[...] = v.T.astype(BF16)
    gate_ref[...] = jax.nn.sigmoid(_dot(h, w_ref[:, o:o + 2 * D_MODEL]))


def _proj(x, g, w_bf16):
    t = x.shape[0]
    tm = min(256, t)
    n_total = w_bf16.shape[1]
    row = lambda n: pl.BlockSpec((tm, n), lambda i: (i, 0))
    sds = lambda n, dt=F32: jax.ShapeDtypeStruct((t, n), dt)
    return pl.pallas_call(
        _proj_kernel,
        grid=(t // tm,),
        in_specs=[row(D_MODEL), _full((1, D_MODEL)), _full((D_MODEL, n_total))],
        out_specs=(row(RWKV_PROJ), row(ATT_QK_W), row(ATT_QK_W), row(ATT_V_W),
                   row(ATT_QK_W), pl.BlockSpec((ATT_V_W, tm), lambda i: (0, i)), row(2 * D_MODEL)),
        out_shape=(sds(RWKV_PROJ), sds(ATT_QK_W), sds(ATT_QK_W), sds(ATT_V_W),
                   sds(ATT_QK_W, BF16), jax.ShapeDtypeStruct((ATT_V_W, t), BF16), sds(2 * D_MODEL)),
        compiler_params=_params(("parallel",)),
        name="in_proj",
    )(x, g, w_bf16)


def _rwkv_pre_kernel(*refs, per_token_prev):
    if per_token_prev:
        z_ref, zp_ref = refs[:2]
        rest = refs[2:]
    else:
        z_ref, halo_ref, s0_ref = refs[:3]
        rest = refs[3:]
    (mu_ref, w0_ref, a0_ref, kk_ref, ka_ref, w2_ref, a2_ref, g2_ref, seg_ref,
     r_out, lw_out, kx_out, v_out, kk_out, kka_out, g_out) = rest
    z = z_ref[...]
    if per_token_prev:
        zp = zp_ref[...]
    else:
        first = jnp.where(pl.program_id(0) == 0, s0_ref[...], halo_ref[7:8, :])
        rowid = lax.broadcasted_iota(I32, z.shape, 0)
        zp = jnp.where(rowid == 0, first, pltpu.roll(z, 1, axis=0))
    zs = z + (zp - z) * mu_ref[...]
    r = zs[:, 0:RWKV_W]
    k = zs[:, RWKV_W:2 * RWKV_W]
    v = zs[:, 2 * RWKV_W:3 * RWKV_W]
    wa = zs[:, 3 * RWKV_W:3 * RWKV_W + DECAY_LORA + AAA_LORA]
    g_lo = zs[:, 3 * RWKV_W + DECAY_LORA + AAA_LORA:]
    wpre = w0_ref[...] + _mm(jnp.tanh(wa), w2_ref[...], passes=3)
    nw = -wpre
    softplus = jnp.maximum(nw, 0.0) + jnp.log1p(jnp.exp(-jnp.abs(nw)))
    w = -softplus - 0.5
    a = jax.nn.sigmoid(a0_ref[...] + _mm(wa, a2_ref[...], passes=3))
    g = _mm(jax.nn.sigmoid(g_lo), g2_ref[...], passes=3)
    kk = k * kk_ref[...]
    nrm = jnp.sqrt(_mm_exact(kk * kk, seg_ref[...]))
    kk = kk / jnp.maximum(nrm, 1e-12)
    r_out[...] = r
    lw_out[...] = -jnp.exp(w)
    kx_out[...] = k * (1.0 + (a - 1.0) * ka_ref[...])
    v_out[...] = v
    kk_out[...] = kk
    kka_out[...] = kk * a
    g_out[...] = g


def _rwkv_pre(z, zprev_or_shift0, per_token_prev, wts):
    t = z.shape[0]
    tm = min(256, t)
    row = lambda n: pl.BlockSpec((tm, n), lambda i: (i, 0))
    vec = lambda n: _full((1, n))
    if per_token_prev:
        lead = [row(RWKV_PROJ), row(RWKV_PROJ)]
        args = [z, zprev_or_shift0]
    else:
        halo = pl.BlockSpec((8, RWKV_PROJ), lambda i: (jnp.maximum(i * (tm // 8) - 1, 0), 0))
        lead = [row(RWKV_PROJ), halo, vec(RWKV_PROJ)]
        args = [z, z, zprev_or_shift0]
    lora_w = DECAY_LORA + AAA_LORA
    return pl.pallas_call(
        functools.partial(_rwkv_pre_kernel, per_token_prev=per_token_prev),
        grid=(t // tm,),
        in_specs=lead + [vec(RWKV_PROJ), vec(RWKV_W), vec(RWKV_W), vec(RWKV_W), vec(RWKV_W),
                         _full((lora_w, RWKV_W)), _full((lora_w, RWKV_W)), _full((GATE_LORA, RWKV_W)),
                         _full((RWKV_W, RWKV_W))],
        out_specs=tuple(row(RWKV_W) for _ in range(7)),
        out_shape=tuple(jax.ShapeDtypeStruct((t, RWKV_W), F32) for _ in range(7)),
        compiler_params=_params(("parallel",)),
        name="rwkv_pre",
    )(*args, wts['mu'], wts['w0'], wts['a0'], wts['k_k'], wts['k_a'], wts['w2p'], wts['a2p'], wts['g2'],
      wts['seg64'])


def _stack(x, lane_head):
    return jnp.concatenate([jnp.where(lane_head == h, x, 0.0) for h in range(4)], axis=0)


def _rwkv_chunk_quad(r, lw, kx, v, kk, ka, s, tri, passes):
    L = CHUNK
    lane_head = lax.broadcasted_iota(I32, (L, QUAD), 1) // RWKV_HEAD
    cs = _mm_exact(tri, lw)
    cs_end = cs[L - 1:L, :]
    e_neg = jnp.exp(-cs)
    at = -kk * jnp.exp(cs - lw)
    bt = ka * e_neg
    kt = kx * e_neg
    rt = r * jnp.exp(cs)
    e_end = jnp.exp(cs_end - cs)
    bh = ka * e_end
    kh = kx * e_end
    s_a, s_b, s_k, s_r = (_stack(x, lane_head) for x in (at, bt, kt, rt))
    s_v, s_bh, s_kh = (_stack(x, lane_head) for x in (v, bh, kh))
    rr = lax.broadcasted_iota(I32, (4 * L, 4 * L), 0) % L
    cc = lax.broadcasted_iota(I32, (4 * L, 4 * L), 1) % L
    strict = rr > cc
    incl = rr >= cc
    mm = functools.partial(_mm, passes=passes)
    n_ab = jnp.where(strict, mm(s_a, s_b, NT), 0.0)
    n_ak = jnp.where(strict, mm(s_a, s_k, NT), 0.0)
    n_rb = jnp.where(incl, mm(s_r, s_b, NT), 0.0)
    n_rk = jnp.where(incl, mm(s_r, s_k, NT), 0.0)
    eye = (lax.broadcasted_iota(I32, (4 * L, 4 * L), 0) == lax.broadcasted_iota(I32, (4 * L, 4 * L), 1))
    x = n_ab
    tinv = jnp.where(eye, 1.0, 0.0) + x
    for _ in range(5):
        x = mm(x, x)
        tinv = tinv + mm(tinv, x)
    q = mm(s_a, s, NT) + mm(n_ak, s_v)
    s_u = mm(tinv, q)
    s_y = mm(s_r, s, NT) + mm(n_rb, s_u) + mm(n_rk, s_v)
    y = jnp.zeros((L, QUAD), F32)
    for h in range(4):
        y = y + jnp.where(lane_head == h, s_y[h * L:(h + 1) * L, :], 0.0)
    s_new = s * jnp.exp(cs_end) + mm(s_u, s_bh, TN) + mm(s_v, s_kh, TN)
    return y, s_new


def _rwkv_chunk_kernel(r_ref, lw_ref, kx_ref, v_ref, kk_ref, ka_ref, tri_ref, y_ref, sout_ref, s_scr,
                       *, nsub, passes):
    @pl.when(pl.program_id(0) == 0)
    def _():
        s_scr[...] = jnp.zeros_like(s_scr)

    tri = tri_ref[...]
    for c in range(nsub):
        rows = slice(c * CHUNK, (c + 1) * CHUNK)
        for qd in range(RWKV_W // QUAD):
            cols = slice(qd * QUAD, (qd + 1) * QUAD)
            y, s_new = _rwkv_chunk_quad(r_ref[rows, cols], lw_ref[rows, cols], kx_ref[rows, cols],
                                        v_ref[rows, cols], kk_ref[rows, cols], ka_ref[rows, cols],
                                        s_scr[qd], tri, passes)
            y_ref[rows, cols] = y
            s_scr[qd] = s_new

    @pl.when(pl.program_id(0) == pl.num_programs(0) - 1)
    def _():
        sout_ref[...] = s_scr[...]


def _rwkv_chunked(r, lw, kx, v, kk, ka, passes=RWKV_PASSES):
    t = r.shape[0]
    nsub = 2 if t % (2 * CHUNK) == 0 else 1
    lb = nsub * CHUNK
    nq = RWKV_W // QUAD
    tri = (jnp.arange(CHUNK)[:, None] >= jnp.arange(CHUNK)[None, :]).astype(BF16)
    row = pl.BlockSpec((lb, RWKV_W), lambda i: (i, 0))
    y, s_bd = pl.pallas_call(
        functools.partial(_rwkv_chunk_kernel, nsub=nsub, passes=passes),
        grid=(t // lb,),
        in_specs=[row] * 6 + [_full((CHUNK, CHUNK))],
        out_specs=(row, _full((nq, QUAD, QUAD))),
        out_shape=(jax.ShapeDtypeStruct((t, RWKV_W), F32), jax.ShapeDtypeStruct((nq, QUAD, QUAD), F32)),
        scratch_shapes=[pltpu.VMEM((nq, QUAD, QUAD), F32)],
        compiler_params=_params(("arbitrary",)),
        name="rwkv_chunked",
    )(r, lw, kx, v, kk, ka, tri)
    blocks = [s_bd[h // 4, (h % 4) * RWKV_HEAD:(h % 4 + 1) * RWKV_HEAD, (h % 4) * RWKV_HEAD:(h % 4 + 1) * RWKV_HEAD]
              for h in range(RWKV_HEADS)]
    return y, jnp.stack(blocks)


def _rwkv_step_kernel(s_ref, r_ref, lw_ref, kx_ref, kk_ref, ka_ref, v_ref, sout_ref, y_ref):
    s = s_ref[...]
    sa = -jnp.sum(s * kk_ref[...], axis=-1, keepdims=True)
    s_new = s * jnp.exp(lw_ref[...]) + sa * ka_ref[...] + v_ref[...] * kx_ref[...]
    sout_ref[...] = s_new
    y_ref[...] = jnp.sum(s_new * r_ref[...], axis=-1, keepdims=True)


def _rwkv_step(state, r, lw, kx, kk, ka, v):
    db = state.shape[0]
    bb = 8 if db % 8 == 0 else 1
    rowv = lambda a: a.reshape(db, RWKV_HEADS, 1, RWKV_HEAD)
    st = pl.BlockSpec((bb, RWKV_HEADS, RWKV_HEAD, RWKV_HEAD), lambda i: (i, 0, 0, 0))
    rv = pl.BlockSpec((bb, RWKV_HEADS, 1, RWKV_HEAD), lambda i: (i, 0, 0, 0))
    cv = pl.BlockSpec((bb, RWKV_HEADS, RWKV_HEAD, 1), lambda i: (i, 0, 0, 0))
    s_new, y = pl.pallas_call(
        _rwkv_step_kernel,
        grid=(db // bb,),
        in_specs=[st, rv, rv, rv, rv, rv, cv],
        out_specs=(st, cv),
        out_shape=(jax.ShapeDtypeStruct(state.shape, F32),
                   jax.ShapeDtypeStruct((db, RWKV_HEADS, RWKV_HEAD, 1), F32)),
        compiler_params=_params(("parallel",)),
        name="rwkv_step",
    )(state, rowv(r), rowv(lw), rowv(kx), rowv(kk), rowv(ka), v.reshape(db, RWKV_HEADS, RWKV_HEAD, 1))
    return y.reshape(db, RWKV_W), s_new


def _attn_prompt_kernel(lam_ref, q_ref, k_ref, vt_ref, bias_ref, o_ref, m_ref, l_ref, acc_ref, *, blk):
    i = pl.program_id(1)
    qt = q_ref[...].T * (HEAD_QK ** -0.5 * LOG2E)
    sub = lax.broadcasted_iota(I32, qt.shape, 0)
    qt2 = jnp.concatenate([jnp.where(sub < HEAD_QK, qt, 0.0), jnp.where(sub >= HEAD_QK, qt, 0.0)],
                          axis=1).astype(BF16)
    m_ref[...] = jnp.full_like(m_ref, -jnp.inf)
    l_ref[...] = jnp.zeros_like(l_ref)
    acc_ref[...] = jnp.zeros_like(acc_ref)

    def update(j, bias):
        start = pl.multiple_of(j * blk, blk)
        s = _dot(k_ref[pl.ds(start, blk), :], qt2)
        if bias is not None:
            s = s + jnp.concatenate([bias, bias], axis=1)
        m_old = m_ref[...]
        m_new = jnp.maximum(m_old, jnp.max(s, axis=0, keepdims=True))
        alpha = jnp.exp2(m_old - m_new)
        p = jnp.exp2(s - m_new)
        l_ref[...] = alpha * l_ref[...] + jnp.sum(p, axis=0, keepdims=True)
        acc_ref[...] = alpha * acc_ref[...] + _dot(vt_ref[:, pl.ds(start, blk)], p.astype(BF16))
        m_ref[...] = m_new

    def far_body(j, carry):
        update(j, None)
        return carry

    lax.fori_loop(0, jnp.maximum(i - 1, 0), far_body, 0)

    @pl.when(i >= 1)
    def _():
        update(i - 1, bias_ref[1])

    update(i, bias_ref[0])
    o = acc_ref[...] / l_ref[...]
    o_ref[...] = (o[:, :blk] - lam_ref[0] * o[:, blk:]).T


def _attn_prompt(lam, q, kb, vt, tiles, blk):
    t = q.shape[0]
    return pl.pallas_call(
        functools.partial(_attn_prompt_kernel, blk=blk),
        grid=(ATT_HEADS, t // blk),
        in_specs=[pl.BlockSpec(memory_space=pltpu.SMEM),
                  pl.BlockSpec((blk, LANES), lambda h, i: (i, h)),
                  pl.BlockSpec((t, LANES), lambda h, i: (0, h)),
                  pl.BlockSpec((HEAD_V, t), lambda h, i: (h, 0)),
                  pl.BlockSpec((None, 2, blk, blk), lambda h, i: (h, 0, 0, 0))],
        out_specs=pl.BlockSpec((blk, LANES), lambda h, i: (i, h)),
        out_shape=jax.ShapeDtypeStruct((t, ATT_V_W), F32),
        scratch_shapes=[pltpu.VMEM((1, 2 * blk), F32), pltpu.VMEM((1, 2 * blk), F32),
                        pltpu.VMEM((HEAD_V, 2 * blk), F32)],
        compiler_params=_params(("parallel", "parallel")),
        name="attn_prompt",
    )(lam, q, kb, vt, tiles)


def _attn_sample_kernel(pt_ref, lam_ref, q_ref, kn_ref, vn_ref, sb_ref, *rest, pages_per_step):
    del pt_ref
    pp = pages_per_step
    page_w = ATT_HEADS * PAGE_SIZE
    k_refs = rest[:pp]
    v_refs = rest[pp:2 * pp]
    o_ref, m_ref, l_ref, acc_ref = rest[2 * pp:]
    g = pl.program_id(1)
    last = g == pl.num_programs(1) - 1
    row = lax.broadcasted_iota(I32, (2 * ATT_HEADS, LANES), 0)
    lane = lax.broadcasted_iota(I32, (2 * ATT_HEADS, LANES), 1)

    def rows8(x4):
        out = jnp.zeros((2 * ATT_HEADS, LANES), F32)
        for h in range(ATT_HEADS):
            out = jnp.where((row >> 1) == h, x4[h:h + 1, :], out)
        return out

    map_lanes = (row & 1) == (lane >= HEAD_QK).astype(I32)
    q8 = jnp.where(map_lanes, rows8(q_ref[0]) * (HEAD_QK ** -0.5 * LOG2E), 0.0)

    @pl.when(g == 0)
    def _():
        s_self = jnp.sum(q8 * rows8(kn_ref[0]), axis=-1, keepdims=True) + sb_ref[:, page_w:page_w + 1]
        m_ref[...] = s_self
        l_ref[...] = jnp.ones_like(l_ref)
        acc_ref[...] = rows8(vn_ref[0])

    qb = q8.astype(BF16)
    ss = [_dot(qb, k_refs[p][...].astype(BF16), NT) for p in range(pp)]
    ss[-1] = ss[-1] + jnp.where(last, sb_ref[:, :page_w], 0.0)
    s_all = jnp.concatenate(ss, axis=-1)
    col_head = lax.broadcasted_iota(I32, s_all.shape, 1) & (ATT_HEADS - 1)
    row_head = lax.broadcasted_iota(I32, s_all.shape, 0) >> 1
    s_all = jnp.where(col_head == row_head, s_all, -jnp.inf)
    m_old = m_ref[...]
    m_new = jnp.maximum(m_old, jnp.max(s_all, axis=-1, keepdims=True))
    alpha = jnp.exp2(m_old - m_new)
    p_all = jnp.exp2(s_all - m_new).astype(BF16)
    l_ref[...] = alpha * l_ref[...] + jnp.sum(p_all.astype(F32), axis=-1, keepdims=True)
    acc = alpha * acc_ref[...]
    for p in range(pp):
        acc = acc + _dot(p_all[:, p * page_w:(p + 1) * page_w], v_refs[p][...].astype(BF16))
    acc_ref[...] = acc
    m_ref[...] = m_new

    @pl.when(last)
    def _():
        o = acc_ref[...] / l_ref[...]
        o_ref[0] = jnp.concatenate([o[2 * h:2 * h + 1, :] - lam_ref[0] * o[2 * h + 1:2 * h + 2, :]
                                    for h in range(ATT_HEADS)], axis=0)


def _attn_sample(lam, q, k_new, v_new, cache_k, cache_v, layer, page_table, sbias):
    db, n_pages = page_table.shape
    pp = 8 if n_pages % 8 == 0 else 1
    page_w = ATT_HEADS * PAGE_SIZE
    tok = pl.BlockSpec((1, ATT_HEADS, LANES), lambda b, g, pt: (b, 0, 0))

    def page_spec(p):
        return pl.BlockSpec((None, None, page_w, LANES), lambda b, g, pt: (layer, pt[b, g * pp + p], 0, 0))

    pages = lambda c: c.reshape(c.shape[0], c.shape[1], page_w, LANES)
    shp = (db, ATT_HEADS, LANES)
    out = pl.pallas_call(
        functools.partial(_attn_sample_kernel, pages_per_step=pp),
        grid_spec=pltpu.PrefetchScalarGridSpec(
            num_scalar_prefetch=1,
            grid=(db, n_pages // pp),
            in_specs=[pl.BlockSpec(memory_space=pltpu.SMEM), tok, tok, tok, _full(sbias.shape)]
                     + [page_spec(p) for p in range(pp)] * 2,
            out_specs=tok,
            scratch_shapes=[pltpu.VMEM((2 * ATT_HEADS, 1), F32), pltpu.VMEM((2 * ATT_HEADS, 1), F32),
                            pltpu.VMEM((2 * ATT_HEADS, LANES), F32)]),
        out_shape=jax.ShapeDtypeStruct(shp, F32),
        compiler_params=_params(("parallel", "arbitrary")),
        name="attn_sample",
    )(page_table, lam, q.reshape(shp), k_new.reshape(shp), v_new.reshape(shp), sbias,
      *([pages(cache_k)] * pp), *([pages(cache_v)] * pp))
    return out.reshape(db, ATT_V_W)


def _merge_kernel(x_ref, y_ref, r_ref, kx_ref, v_ref, g_ref, o_ref, gate_ref,
                  lnw_ref, lnb_ref, rk_ref, sub_ref, seg_ref, wbr_ref, wba_ref, wout_ref, nffn_ref,
                  x1_ref, xnt_ref, *, attn_scale):
    seg = seg_ref[...]
    y = y_ref[...]
    inv_n = 1.0 / RWKV_HEAD
    mean = _mm_exact(y, seg) * inv_n
    yc = y - mean
    var = _mm_exact(yc * yc, seg) * inv_n
    yn = yc * lax.rsqrt(var + RWKV_LN_EPS) * lnw_ref[...] + lnb_ref[...]
    v = v_ref[...]
    bonus = _mm_exact(r_ref[...] * kx_ref[...] * rk_ref[...], seg) * v
    y_rwkv = (yn + bonus) * g_ref[...]
    o = o_ref[...]
    parts = []
    for h in range(ATT_HEADS):
        oh = o[:, h * HEAD_V:(h + 1) * HEAD_V]
        parts.append(_rms(oh, sub_ref[...], SUBLN_EPS) * attn_scale)
    y_attn = jnp.concatenate(parts, axis=-1)
    gates = gate_ref[...]
    merged = (gates[:, :D_MODEL] * _mm(y_rwkv, wbr_ref[...])
              + gates[:, D_MODEL:] * _mm(y_attn, wba_ref[...]))
    x1 = x_ref[...] + _mm(merged, wout_ref[...])
    x1_ref[...] = x1
    xnt_ref[...] = _rms(x1, nffn_ref[...], NORM_EPS).T.astype(BF16)


def _merge(x, y, r, kx, v, g, o, gates, wts, attn_scale):
    t = x.shape[0]
    tm = min(256, t)
    row = lambda n: pl.BlockSpec((tm, n), lambda i: (i, 0))
    vec = lambda n: _full((1, n))
    return pl.pallas_call(
        functools.partial(_merge_kernel, attn_scale=attn_scale),
        grid=(t // tm,),
        in_specs=[row(D_MODEL)] + [row(RWKV_W)] * 6 + [row(2 * D_MODEL),
                  vec(RWKV_W), vec(RWKV_W), vec(RWKV_W), vec(HEAD_V), _full((RWKV_W, RWKV_W)),
                  _full((RWKV_W, D_MODEL)), _full((ATT_V_W, D_MODEL)), _full((D_MODEL, D_MODEL)),
                  vec(D_MODEL)],
        out_specs=(row(D_MODEL), pl.BlockSpec((D_MODEL, tm), lambda i: (0, i))),
        out_shape=(jax.ShapeDtypeStruct((t, D_MODEL), F32), jax.ShapeDtypeStruct((D_MODEL, t), BF16)),
        compiler_params=_params(("parallel",)),
        name="merge",
    )(x, y, r, kx, v, g, o, gates, wts['ln_w'], wts['ln_b'], wts['r_k'], wts['subln'], wts['seg64'],
      wts['w_br_rwkv'], wts['w_br_attn'], wts['w_out'], wts['norm_ffn'])


def _top_sorted(vals, k):
    rows = []
    for _ in range(k):
        mx = jnp.max(vals, axis=0, keepdims=True)
        rows.append(mx)
        vals = jnp.where(vals == mx, -jnp.inf, vals)
    return rows


def _peer_route_kernel(xnt_ref, wq_ref, k1_ref, k2_ref, rank_ref, nsel_ref, e1_ref, e2_ref):
    half = PEER_TOPK // 2
    qt = _dot(wq_ref[...], xnt_ref[...])
    k1 = k1_ref[...].astype(BF16)
    k2 = k2_ref[...].astype(BF16)
    for h in range(PEER_HEADS):
        base = h * PEER_QDIM
        s1 = _dot(k1, qt[base:base + PEER_HALF, :].astype(BF16))
        s2 = _dot(k2, qt[base + PEER_HALF:base + PEER_QDIM, :].astype(BF16))
        v1 = _top_sorted(s1, PEER_TOPK)
        v2 = _top_sorted(s2, PEER_TOPK)
        v1m = jnp.concatenate(v1, axis=0)
        v2m = jnp.concatenate(v2, axis=0)
        groups = ([v1[0] + v2m] + [v1[i] + v2m[:half] for i in range(1, half)] + [v1m[half:] + v2[0]])
        cand = jnp.concatenate(groups, axis=0)
        tau = _top_sorted(cand, PEER_TOPK)[-1]
        top = v1[0] + v2[0]
        z = jnp.sum(jnp.where(cand >= tau, jnp.exp(cand - top), 0.0), axis=0, keepdims=True)
        count = lambda grp: jnp.sum(jnp.where(grp >= tau, 1.0, 0.0), axis=0, keepdims=True)
        n_rank = [count(groups[i]) for i in range(half)]
        tail = jnp.where(groups[half] >= tau, 1.0, 0.0)
        n_rank += [tail[i:i + 1, :] for i in range(half)]
        nsel = jnp.zeros(s1.shape, F32)
        rank2 = jnp.full(s2.shape, float(PEER_TOPK), F32)
        for i in range(PEER_TOPK):
            nsel = jnp.where(s1 == v1[i], n_rank[i], nsel)
            rank2 = jnp.where(s2 == v2[i], float(i), rank2)
        rank_ref[h] = rank2.astype(BF16)
        nsel_ref[h] = nsel
        e1_ref[h] = jnp.exp(s1 - v1[0]) / z
        e2_ref[h] = jnp.exp(s2 - v2[0]).astype(BF16)


def _peer_tokens(t):
    return min(512, t)


def _peer_route(xnt, wts):
    t = xnt.shape[1]
    tb = _peer_tokens(t)
    col = pl.BlockSpec((PEER_HEADS, N_KEYS, tb), lambda i: (0, 0, i))
    big = lambda dt: jax.ShapeDtypeStruct((PEER_HEADS, N_KEYS, t), dt)
    return pl.pallas_call(
        _peer_route_kernel,
        grid=(t // tb,),
        in_specs=[pl.BlockSpec((D_MODEL, tb), lambda i: (0, i)),
                  _full((PEER_HEADS * PEER_QDIM, D_MODEL)),
                  _full((N_KEYS, PEER_HALF)), _full((N_KEYS, PEER_HALF))],
        out_specs=(col, col, col, col),
        out_shape=(big(BF16), big(F32), big(F32), big(BF16)),
        compiler_params=_params(("parallel",)),
        name="peer_route",
    )(xnt, wts['wqT'], wts['peer_k1'], wts['peer_k2'])


ROW_TILE = 16


def _peer_dense_kernel(xnt_ref, rank_ref, nsel_ref, e1_ref, e2_ref, eu_ref, evt_ref, x1_ref,
                       o_ref, yt_ref, act_ref, gated_ref, bn_ref, be1_ref, *, eb):
    j = pl.program_id(1)

    @pl.when(j == 0)
    def _():
        yt_ref[...] = jnp.zeros_like(yt_ref)

    xnt = xnt_ref[...]
    half = eb // 2
    for c in range(2):
        u = _dot(eu_ref[c * half:(c + 1) * half, :], xnt)
        act_ref[c * half:(c + 1) * half, :] = (0.5 * u * (1.0 + lax.erf(u * (2.0 ** -0.5)))).astype(BF16)
    groups = eb // N_KEYS
    tb = xnt.shape[1]
    zero = jnp.zeros((ROW_TILE, tb), BF16)
    for a in range(groups):
        i1 = j * groups + a
        for h in range(PEER_HEADS):
            bn_ref[a, h] = jnp.broadcast_to(nsel_ref[h, pl.ds(i1, 1), :], (ROW_TILE, tb)).astype(BF16)
            be1_ref[a, h] = jnp.broadcast_to(e1_ref[h, pl.ds(i1, 1), :], (ROW_TILE, tb)).astype(BF16)
        for rt in range(N_KEYS // ROW_TILE):
            rows = slice(rt * ROW_TILE, (rt + 1) * ROW_TILE)
            w = None
            for h in range(PEER_HEADS):
                wh = jnp.where(rank_ref[h, rows, :] < bn_ref[a, h], e2_ref[h, rows, :], zero) * be1_ref[a, h]
                w = wh if w is None else w + wh
            erows = slice(a * N_KEYS + rt * ROW_TILE, a * N_KEYS + (rt + 1) * ROW_TILE)
            gated_ref[erows, :] = w * act_ref[erows, :]
    yt_ref[...] += _dot(evt_ref[...], gated_ref[...])

    @pl.when(j == pl.num_programs(1) - 1)
    def _():
        o_ref[...] = x1_ref[...] + yt_ref[...].T


def _peer_dense(xnt, route, x1, wts):
    t = x1.shape[0]
    tb = _peer_tokens(t)
    eb = 512
    col = pl.BlockSpec((PEER_HEADS, N_KEYS, tb), lambda i, j: (0, 0, i))
    return pl.pallas_call(
        functools.partial(_peer_dense_kernel, eb=eb),
        grid=(t // tb, N_EXPERTS // eb),
        in_specs=[pl.BlockSpec((D_MODEL, tb), lambda i, j: (0, i)), col, col, col, col,
                  pl.BlockSpec((eb, D_MODEL), lambda i, j: (j, 0)),
                  pl.BlockSpec((D_MODEL, eb), lambda i, j: (0, j)),
                  pl.BlockSpec((tb, D_MODEL), lambda i, j: (i, 0))],
        out_specs=pl.BlockSpec((tb, D_MODEL), lambda i, j: (i, 0)),
        out_shape=jax.ShapeDtypeStruct((t, D_MODEL), F32),
        scratch_shapes=[pltpu.VMEM((D_MODEL, tb), F32), pltpu.VMEM((eb, tb), BF16), pltpu.VMEM((eb, tb), BF16),
                        pltpu.VMEM((eb // N_KEYS, PEER_HEADS, ROW_TILE, tb), BF16),
                        pltpu.VMEM((eb // N_KEYS, PEER_HEADS, ROW_TILE, tb), BF16)],
        compiler_params=_params(("parallel", "arbitrary")),
        name="peer_dense",
    )(xnt, *route, wts['eu'], wts['evT'], x1)


def _ple_kernel(x_ref, pe_ref, wple_ref, nple_ref, wgate_ref, nfin_ref, o_ref, *, final_norm):
    x = x_ref[...]
    gate = jax.nn.sigmoid(_mm(_rms(x, nple_ref[...], NORM_EPS), wgate_ref[...]))
    x = x + _mm(pe_ref[...], wple_ref[...]) * gate
    o_ref[...] = _rms(x, nfin_ref[...], NORM_EPS) if final_norm else x


def _ple(x, pe, wts, final_norm):
    t = x.shape[0]
    tm = min(256, t)
    row = lambda n: pl.BlockSpec((tm, n), lambda i: (i, 0))
    return pl.pallas_call(
        functools.partial(_ple_kernel, final_norm=final_norm),
        grid=(t // tm,),
        in_specs=[row(D_MODEL), row(PLE_DIM), _full((PLE_DIM, D_MODEL)), _full((1, D_MODEL)),
                  _full((D_MODEL, D_MODEL)), _full((1, D_MODEL))],
        out_specs=row(D_MODEL),
        out_shape=jax.ShapeDtypeStruct((t, D_MODEL), F32),
        compiler_params=_params(("parallel",)),
        name="ple_final",
    )(x, pe, wts['w_ple'], wts['norm_ple'], wts['w_ple_gate'], wts['norm_final'])


def _token_stages(x, pe, y_rwkv_raw, pre, o_attn, gates, wts, attn_scale, final_norm):
    r, _, kx, v, _, _, g = pre
    x1, xnt = _merge(x, y_rwkv_raw, r, kx, v, g, o_attn, gates, wts, attn_scale)
    x2 = _peer_dense(xnt, _peer_route(xnt, wts), x1, wts)
    return _ple(x2, pe, wts, final_norm)


def kernel(x_prompt, x_sample, cache_k, cache_v, state_wkv, state_shift, page_table, p_prompt, p_sample,
           norm_mix, w_in, rwkv_mu, rwkv_w0, rwkv_w2, rwkv_a0, rwkv_a2, rwkv_g2, rwkv_k_k, rwkv_k_a,
           rwkv_r_k, rwkv_ln_w, rwkv_ln_b, attn_lq1, attn_lk1, attn_lq2, attn_lk2, attn_subln, rel_bias,
           w_br_rwkv, w_br_attn, w_out, norm_ffn, peer_wq, peer_k1, peer_k2, peer_u, peer_v,
           norm_ple, w_ple, w_ple_gate, norm_final):
    depth = w_in.shape[0]
    bsz, seq = x_prompt.shape[0], x_prompt.shape[1]
    db, dseq = x_sample.shape[0], x_sample.shape[1]
    assert bsz == 1 and dseq == 1, "prompt batch and decode length are fixed at 1"
    blk = next((b for b in (512, 256) if seq % b == 0), seq)
    xp = x_prompt.reshape(seq, D_MODEL)
    xs = x_sample.reshape(db, D_MODEL)
    tiles, sbias = _bias_tiles(rel_bias, blk)
    seg64 = (jnp.arange(RWKV_W)[:, None] // RWKV_HEAD == jnp.arange(RWKV_W)[None, :] // RWKV_HEAD).astype(BF16)
    row = lambda a: a.reshape(1, -1)
    outs = [[] for _ in range(8)]
    for l in range(depth):
        zpad = jnp.zeros((DECAY_LORA, RWKV_W), F32)
        wts = {
            'mu': row(rwkv_mu[l]), 'w0': row(rwkv_w0[l]), 'a0': row(rwkv_a0[l]),
            'k_k': row(rwkv_k_k[l]), 'k_a': row(rwkv_k_a[l]),
            'w2p': jnp.concatenate([rwkv_w2[l], zpad], axis=0),
            'a2p': jnp.concatenate([zpad, rwkv_a2[l]], axis=0),
            'g2': rwkv_g2[l], 'seg64': seg64,
            'ln_w': row(rwkv_ln_w[l]), 'ln_b': row(rwkv_ln_b[l]), 'r_k': row(rwkv_r_k[l]),
            'subln': row(attn_subln[l]),
            'w_br_rwkv': w_br_rwkv[l].astype(BF16), 'w_br_attn': w_br_attn[l].astype(BF16),
            'w_out': w_out[l].astype(BF16), 'norm_ffn': row(norm_ffn[l]),
            'wqT': peer_wq[l].T.astype(BF16),
            'peer_k1': peer_k1[l], 'peer_k2': peer_k2[l],
            'eu': peer_u[l].astype(BF16), 'evT': peer_v[l].T.astype(BF16),
            'w_ple': w_ple[l].astype(BF16), 'norm_ple': row(norm_ple[l]),
            'w_ple_gate': w_ple_gate[l].astype(BF16), 'norm_final': row(norm_final),
        }
        lam_init = 0.8 - 0.6 * math.exp(-0.3 * l)
        lam = (jnp.exp(jnp.sum(attn_lq1[l] * attn_lk1[l])) - jnp.exp(jnp.sum(attn_lq2[l] * attn_lk2[l]))
               + lam_init).reshape(1).astype(F32)
        w_in_b = w_in[l].astype(BF16)
        g_mix = row(norm_mix[l])

        zr, q, k, v, kb, vt, gates = _proj(xp, g_mix, w_in_b)
        pre = _rwkv_pre(zr, jnp.zeros((1, RWKV_PROJ), F32), False, wts)
        y_raw, wkv_p = _rwkv_chunked(*pre[:6])
        o = _attn_prompt(lam, q, kb, vt, tiles, blk)
        xp = _token_stages(xp, p_prompt[l].reshape(seq, PLE_DIM), y_raw, pre, o, gates, wts,
                           1.0 - lam_init, l == depth - 1)
        outs[0].append(k.reshape(1, seq, ATT_HEADS, 2 * HEAD_QK))
        outs[1].append(v.reshape(1, seq, ATT_HEADS, HEAD_V))
        outs[4].append(wkv_p[None])
        outs[6].append(zr[seq - 1:seq, :])

        zr, q, k, v, _, _, gates = _proj(xs, g_mix, w_in_b)
        pre = _rwkv_pre(zr, state_shift[l], True, wts)
        r, lw, kx, vv, kk, ka, _ = pre
        y_raw, wkv_s = _rwkv_step(state_wkv[l], r, lw, kx, kk, ka, vv)
        o = _attn_sample(lam, q, k, v, cache_k, cache_v, l, page_table, sbias)
        xs = _token_stages(xs, p_sample[l].reshape(db, PLE_DIM), y_raw, pre, o, gates, wts,
                           1.0 - lam_init, l == depth - 1)
        outs[2].append(k.reshape(db, 1, ATT_HEADS, 2 * HEAD_QK))
        outs[3].append(v.reshape(db, 1, ATT_HEADS, HEAD_V))
        outs[5].append(wkv_s)
        outs[7].append(zr)
    y_prompt = xp.reshape(1, seq, D_MODEL)
    y_sample = xs.reshape(db, 1, D_MODEL)
    st = lambda i: jnp.stack(outs[i])
    return (y_prompt, y_sample, st(0), st(1), st(2), st(3), st(4), st(5), st(6), st(7))
```

```python
import functools
import math

import jax
import jax.numpy as jnp
from jax import lax
from jax.experimental import pallas as pl
from jax.experimental.pallas import tpu as pltpu

F32 = jnp.float32
BF16 = jnp.bfloat16
I32 = jnp.int32

D_MODEL = 1024
PAGE_SIZE = 128
RWKV_HEADS = 8
RWKV_HEAD = 64
RWKV_W = RWKV_HEADS * RWKV_HEAD
DECAY_LORA = 64
AAA_LORA = 64
GATE_LORA = 128
RWKV_PROJ = 3 * RWKV_W + DECAY_LORA + AAA_LORA + GATE_LORA
RWKV_LN_EPS = 64e-5
ATT_HEADS = 4
HEAD_QK = 64
HEAD_V = 2 * HEAD_QK
ATT_QK_W = ATT_HEADS * 2 * HEAD_QK
ATT_V_W = ATT_HEADS * HEAD_V
SUBLN_EPS = 1e-5
N_BUCKETS = 32
MAX_EXACT = N_BUCKETS // 2
MAX_DISTANCE = 128
PEER_HEADS = 8
PEER_QDIM = 256
PEER_HALF = PEER_QDIM // 2
N_KEYS = 128
N_EXPERTS = N_KEYS * N_KEYS
PEER_TOPK = 16
PLE_DIM = 256
NORM_EPS = 1e-6
NEG_INF = -1e30

LANES = 128
QUAD = 4 * RWKV_HEAD
CHUNK = 64
VMEM_LIMIT = 56 * 1024 * 1024
LOG2E = math.log2(math.e)
RWKV_PASSES = 1

NN = (((1,), (0,)), ((), ()))
NT = (((1,), (1,)), ((), ()))
TN = (((0,), (0,)), ((), ()))


def _dot(a, b, dims=NN):
    return lax.dot_general(a, b, dims, preferred_element_type=F32)


def _split(x, n):
    parts = []
    for _ in range(n):
        p = x.astype(BF16)
        parts.append(p)
        x = x - p.astype(F32)
    return parts


def _mm(a, b, dims=NN, passes=1):
    if passes == 1:
        return _dot(a.astype(BF16), b.astype(BF16), dims)
    ah, al = _split(a, 2)
    bh, bl = _split(b, 2)
    return _dot(ah, bh, dims) + (_dot(ah, bl, dims) + _dot(al, bh, dims))


def _mm_exact(a, b, dims=NN):
    a1, a2, a3 = _split(a, 3)
    return _dot(a1, b, dims) + (_dot(a2, b, dims) + _dot(a3, b, dims))


def _rms(x, g, eps):
    return x * lax.rsqrt(jnp.mean(x * x, axis=-1, keepdims=True) + eps) * g


def _full(shape):
    return pl.BlockSpec(shape, lambda *_: (0,) * len(shape))


def _params(sem):
    return pltpu.CompilerParams(dimension_semantics=sem, vmem_limit_bytes=VMEM_LIMIT)


def _t5_bucket(n):
    nf = jnp.maximum(n, 1).astype(F32)
    large = MAX_EXACT + (jnp.log(nf / MAX_EXACT) / math.log(MAX_DISTANCE / MAX_EXACT)
                         * (N_BUCKETS - MAX_EXACT)).astype(I32)
    large = jnp.minimum(large, N_BUCKETS - 1)
    return jnp.where(n < MAX_EXACT, n, large)


def _bias_kernel(tab_ref, tiles_ref, sb_ref, *, blk):
    r = lax.broadcasted_iota(I32, (blk, blk), 0)
    c = lax.broadcasted_iota(I32, (blk, blk), 1)
    for t in range(2):
        dist = c - r + t * blk
        bucket = _t5_bucket(jnp.maximum(dist, 0))
        for h in range(ATT_HEADS):
            far = tab_ref[N_BUCKETS - 1, h]
            val = jnp.zeros((blk, blk), F32)
            for b in range(N_BUCKETS - 1):
                val = jnp.where(bucket == b, (tab_ref[b, h] - far) * LOG2E, val)
            if t == 0:
                val = jnp.where(dist >= 0, val, NEG_INF)
            tiles_ref[h, t] = val
    page_w = ATT_HEADS * PAGE_SIZE
    sr = lax.broadcasted_iota(I32, (2 * ATT_HEADS, page_w + LANES), 0)
    sc = lax.broadcasted_iota(I32, (2 * ATT_HEADS, page_w + LANES), 1)
    sbucket = _t5_bucket(jnp.where(sc < page_w, PAGE_SIZE - (sc >> 2), 0))
    sval = jnp.zeros(sr.shape, F32)
    for h in range(ATT_HEADS):
        far = tab_ref[N_BUCKETS - 1, h]
        hval = jnp.zeros(sr.shape, F32)
        for b in range(N_BUCKETS - 1):
            hval = jnp.where(sbucket == b, (tab_ref[b, h] - far) * LOG2E, hval)
        sval = jnp.where((sr >> 1) == h, hval, sval)
    sb_ref[...] = sval


def _bias_tiles(rel_bias, blk):
    return pl.pallas_call(
        functools.partial(_bias_kernel, blk=blk),
        out_shape=(jax.ShapeDtypeStruct((ATT_HEADS, 2, blk, blk), F32),
                   jax.ShapeDtypeStruct((2 * ATT_HEADS, ATT_HEADS * PAGE_SIZE + LANES), F32)),
        in_specs=[pl.BlockSpec(memory_space=pltpu.SMEM)],
        name="t5_bias_tiles",
    )(rel_bias)


def _proj_kernel(x_ref, g_ref, w_ref, zr_ref, q_ref, k_ref, v_ref, kb_ref, vt_ref, gate_ref):
    h = _rms(x_ref[...], g_ref[...], NORM_EPS).astype(BF16)
    o = 0
    zr_ref[...] = _dot(h, w_ref[:, o:o + RWKV_PROJ]); o += RWKV_PROJ
    q_ref[...] = _dot(h, w_ref[:, o:o + ATT_QK_W]); o += ATT_QK_W
    k = _dot(h, w_ref[:, o:o + ATT_QK_W]); o += ATT_QK_W
    v = _dot(h, w_ref[:, o:o + ATT_V_W]); o += ATT_V_W
    k_ref[...] = k
    v_ref[...] = v
    kb_ref[...] = k.astype(BF16)
    vt_ref[...] = v.T.astype(BF16)
    gate_ref[...] = jax.nn.sigmoid(_dot(h, w_ref[:, o:o + 2 * D_MODEL]))


def _proj(x, g, w_bf16):
    t = x.shape[0]
    tm = min(256, t)
    n_total = w_bf16.shape[1]
    row = lambda n: pl.BlockSpec((tm, n), lambda i: (i, 0))
    sds = lambda n, dt=F32: jax.ShapeDtypeStruct((t, n), dt)
    return pl.pallas_call(
        _proj_kernel,
        grid=(t // tm,),
        in_specs=[row(D_MODEL), _full((1, D_MODEL)), _full((D_MODEL, n_total))],
        out_specs=(row(RWKV_PROJ), row(ATT_QK_W), row(ATT_QK_W), row(ATT_V_W),
                   row(ATT_QK_W), pl.BlockSpec((ATT_V_W, tm), lambda i: (0, i)), row(2 * D_MODEL)),
        out_shape=(sds(RWKV_PROJ), sds(ATT_QK_W), sds(ATT_QK_W), sds(ATT_V_W),
                   sds(ATT_QK_W, BF16), jax.ShapeDtypeStruct((ATT_V_W, t), BF16), sds(2 * D_MODEL)),
        compiler_params=_params(("parallel",)),
        name="in_proj",
    )(x, g, w_bf16)


def _rwkv_pre_kernel(*refs, per_token_prev):
    if per_token_prev:
        z_ref, zp_ref = refs[:2]
        rest = refs[2:]
    else:
        z_ref, halo_ref, s0_ref = refs[:3]
        rest = refs[3:]
    (mu_ref, w0_ref, a0_ref, kk_ref, ka_ref, w2_ref, a2_ref, g2_ref, seg_ref,
     r_out, lw_out, kx_out, v_out, kk_out, kka_out, g_out) = rest
    z = z_ref[...]
    if per_token_prev:
        zp = zp_ref[...]
    else:
        first = jnp.where(pl.program_id(0) == 0, s0_ref[...], halo_ref[7:8, :])
        rowid = lax.broadcasted_iota(I32, z.shape, 0)
        zp = jnp.where(rowid == 0, first, pltpu.roll(z, 1, axis=0))
    zs = z + (zp - z) * mu_ref[...]
    r = zs[:, 0:RWKV_W]
    k = zs[:, RWKV_W:2 * RWKV_W]
    v = zs[:, 2 * RWKV_W:3 * RWKV_W]
    wa = zs[:, 3 * RWKV_W:3 * RWKV_W + DECAY_LORA + AAA_LORA]
    g_lo = zs[:, 3 * RWKV_W + DECAY_LORA + AAA_LORA:]
    wpre = w0_ref[...] + _mm(jnp.tanh(wa), w2_ref[...], passes=3)
    nw = -wpre
    softplus = jnp.maximum(nw, 0.0) + jnp.log1p(jnp.exp(-jnp.abs(nw)))
    w = -softplus - 0.5
    a = jax.nn.sigmoid(a0_ref[...] + _mm(wa, a2_ref[...], passes=3))
    g = _mm(jax.nn.sigmoid(g_lo), g2_ref[...], passes=3)
    kk = k * kk_ref[...]
    nrm = jnp.sqrt(_mm_exact(kk * kk, seg_ref[...]))
    kk = kk / jnp.maximum(nrm, 1e-12)
    r_out[...] = r
    lw_out[...] = -jnp.exp(w)
    kx_out[...] = k * (1.0 + (a - 1.0) * ka_ref[...])
    v_out[...] = v
    kk_out[...] = kk
    kka_out[...] = kk * a
    g_out[...] = g


def _rwkv_pre(z, zprev_or_shift0, per_token_prev, wts):
    t = z.shape[0]
    tm = min(256, t)
    row = lambda n: pl.BlockSpec((tm, n), lambda i: (i, 0))
    vec = lambda n: _full((1, n))
    if per_token_prev:
        lead = [row(RWKV_PROJ), row(RWKV_PROJ)]
        args = [z, zprev_or_shift0]
    else:
        halo = pl.BlockSpec((8, RWKV_PROJ), lambda i: (jnp.maximum(i * (tm // 8) - 1, 0), 0))
        lead = [row(RWKV_PROJ), halo, vec(RWKV_PROJ)]
        args = [z, z, zprev_or_shift0]
    lora_w = DECAY_LORA + AAA_LORA
    return pl.pallas_call(
        functools.partial(_rwkv_pre_kernel, per_token_prev=per_token_prev),
        grid=(t // tm,),
        in_specs=lead + [vec(RWKV_PROJ), vec(RWKV_W), vec(RWKV_W), vec(RWKV_W), vec(RWKV_W),
                         _full((lora_w, RWKV_W)), _full((lora_w, RWKV_W)), _full((GATE_LORA, RWKV_W)),
                         _full((RWKV_W, RWKV_W))],
        out_specs=tuple(row(RWKV_W) for _ in range(7)),
        out_shape=tuple(jax.ShapeDtypeStruct((t, RWKV_W), F32) for _ in range(7)),
        compiler_params=_params(("parallel",)),
        name="rwkv_pre",
    )(*args, wts['mu'], wts['w0'], wts['a0'], wts['k_k'], wts['k_a'], wts['w2p'], wts['a2p'], wts['g2'],
      wts['seg64'])


def _stack(x, lane_head):
    return jnp.concatenate([jnp.where(lane_head == h, x, 0.0) for h in range(4)], axis=0)


def _rwkv_chunk_quad(r, lw, kx, v, kk, ka, s, tri, passes):
    L = CHUNK
    lane_head = lax.broadcasted_iota(I32, (L, QUAD), 1) // RWKV_HEAD
    cs = _mm_exact(tri, lw)
    cs_end = cs[L - 1:L, :]
    e_neg = jnp.exp(-cs)
    at = -kk * jnp.exp(cs - lw)
    bt = ka * e_neg
    kt = kx * e_neg
    rt = r * jnp.exp(cs)
    e_end = jnp.exp(cs_end - cs)
    bh = ka * e_end
    kh = kx * e_end
    s_a, s_b, s_k, s_r = (_stack(x, lane_head) for x in (at, bt, kt, rt))
    s_v, s_bh, s_kh = (_stack(x, lane_head) for x in (v, bh, kh))
    rr = lax.broadcasted_iota(I32, (4 * L, 4 * L), 0) % L
    cc = lax.broadcasted_iota(I32, (4 * L, 4 * L), 1) % L
    strict = rr > cc
    incl = rr >= cc
    mm = functools.partial(_mm, passes=passes)
    n_ab = jnp.where(strict, mm(s_a, s_b, NT), 0.0)
    n_ak = jnp.where(strict, mm(s_a, s_k, NT), 0.0)
    n_rb = jnp.where(incl, mm(s_r, s_b, NT), 0.0)
    n_rk = jnp.where(incl, mm(s_r, s_k, NT), 0.0)
    eye = (lax.broadcasted_iota(I32, (4 * L, 4 * L), 0) == lax.broadcasted_iota(I32, (4 * L, 4 * L), 1))
    x = n_ab
    tinv = jnp.where(eye, 1.0, 0.0) + x
    for _ in range(5):
        x = mm(x, x)
        tinv = tinv + mm(tinv, x)
    q = mm(s_a, s, NT) + mm(n_ak, s_v)
    s_u = mm(tinv, q)
    s_y = mm(s_r, s, NT) + mm(n_rb, s_u) + mm(n_rk, s_v)
    y = jnp.zeros((L, QUAD), F32)
    for h in range(4):
        y = y + jnp.where(lane_head == h, s_y[h * L:(h + 1) * L, :], 0.0)
    s_new = s * jnp.exp(cs_end) + mm(s_u, s_bh, TN) + mm(s_v, s_kh, TN)
    return y, s_new


def _rwkv_chunk_kernel(r_ref, lw_ref, kx_ref, v_ref, kk_ref, ka_ref, tri_ref, y_ref, sout_ref, s_scr,
                       *, nsub, passes):
    @pl.when(pl.program_id(0) == 0)
    def _():
        s_scr[...] = jnp.zeros_like(s_scr)

    tri = tri_ref[...]
    for c in range(nsub):
        rows = slice(c * CHUNK, (c + 1) * CHUNK)
        for qd in range(RWKV_W // QUAD):
            cols = slice(qd * QUAD, (qd + 1) * QUAD)
            y, s_new = _rwkv_chunk_quad(r_ref[rows, cols], lw_ref[rows, cols], kx_ref[rows, cols],
                                        v_ref[rows, cols], kk_ref[rows, cols], ka_ref[rows, cols],
                                        s_scr[qd], tri, passes)
            y_ref[rows, cols] = y
            s_scr[qd] = s_new

    @pl.when(pl.program_id(0) == pl.num_programs(0) - 1)
    def _():
        sout_ref[...] = s_scr[...]


def _rwkv_chunked(r, lw, kx, v, kk, ka, passes=RWKV_PASSES):
    t = r.shape[0]
    nsub = 2 if t % (2 * CHUNK) == 0 else 1
    lb = nsub * CHUNK
    nq = RWKV_W // QUAD
    tri = (jnp.arange(CHUNK)[:, None] >= jnp.arange(CHUNK)[None, :]).astype(BF16)
    row = pl.BlockSpec((lb, RWKV_W), lambda i: (i, 0))
    y, s_bd = pl.pallas_call(
        functools.partial(_rwkv_chunk_kernel, nsub=nsub, passes=passes),
        grid=(t // lb,),
        in_specs=[row] * 6 + [_full((CHUNK, CHUNK))],
        out_specs=(row, _full((nq, QUAD, QUAD))),
        out_shape=(jax.ShapeDtypeStruct((t, RWKV_W), F32), jax.ShapeDtypeStruct((nq, QUAD, QUAD), F32)),
        scratch_shapes=[pltpu.VMEM((nq, QUAD, QUAD), F32)],
        compiler_params=_params(("arbitrary",)),
        name="rwkv_chunked",
    )(r, lw, kx, v, kk, ka, tri)
    blocks = [s_bd[h // 4, (h % 4) * RWKV_HEAD:(h % 4 + 1) * RWKV_HEAD, (h % 4) * RWKV_HEAD:(h % 4 + 1) * RWKV_HEAD]
              for h in range(RWKV_HEADS)]
    return y, jnp.stack(blocks)


def _rwkv_step_kernel(s_ref, r_ref, lw_ref, kx_ref, kk_ref, ka_ref, v_ref, sout_ref, y_ref):
    s = s_ref[...]
    sa = -jnp.sum(s * kk_ref[...], axis=-1, keepdims=True)
    s_new = s * jnp.exp(lw_ref[...]) + sa * ka_ref[...] + v_ref[...] * kx_ref[...]
    sout_ref[...] = s_new
    y_ref[...] = jnp.sum(s_new * r_ref[...], axis=-1, keepdims=True)


def _rwkv_step(state, r, lw, kx, kk, ka, v):
    db = state.shape[0]
    bb = 8 if db % 8 == 0 else 1
    rowv = lambda a: a.reshape(db, RWKV_HEADS, 1, RWKV_HEAD)
    st = pl.BlockSpec((bb, RWKV_HEADS, RWKV_HEAD, RWKV_HEAD), lambda i: (i, 0, 0, 0))
    rv = pl.BlockSpec((bb, RWKV_HEADS, 1, RWKV_HEAD), lambda i: (i, 0, 0, 0))
    cv = pl.BlockSpec((bb, RWKV_HEADS, RWKV_HEAD, 1), lambda i: (i, 0, 0, 0))
    s_new, y = pl.pallas_call(
        _rwkv_step_kernel,
        grid=(db // bb,),
        in_specs=[st, rv, rv, rv, rv, rv, cv],
        out_specs=(st, cv),
        out_shape=(jax.ShapeDtypeStruct(state.shape, F32),
                   jax.ShapeDtypeStruct((db, RWKV_HEADS, RWKV_HEAD, 1), F32)),
        compiler_params=_params(("parallel",)),
        name="rwkv_step",
    )(state, rowv(r), rowv(lw), rowv(kx), rowv(kk), rowv(ka), v.reshape(db, RWKV_HEADS, RWKV_HEAD, 1))
    return y.reshape(db, RWKV_W), s_new


def _attn_prompt_kernel(lam_ref, q_ref, k_ref, vt_ref, bias_ref, o_ref, m_ref, l_ref, acc_ref, s_ref, *, blk):
    i = pl.program_id(1)
    qt = q_ref[...].T * (HEAD_QK ** -0.5 * LOG2E)
    sub = lax.broadcasted_iota(I32, qt.shape, 0)
    qt2 = jnp.concatenate([jnp.where(sub < HEAD_QK, qt, 0.0), jnp.where(sub >= HEAD_QK, qt, 0.0)],
                          axis=1).astype(BF16)
    m_ref[...] = jnp.full_like(m_ref, -jnp.inf)
    l_ref[...] = jnp.zeros_like(l_ref)
    acc_ref[...] = jnp.zeros_like(acc_ref)

    def scores(j, slot):
        start = pl.multiple_of(j * blk, blk)
        s_ref[slot] = _dot(k_ref[pl.ds(start, blk), :], qt2)

    def step(j, slot, bias, prefetch):
        if prefetch:
            scores(j + 1, 1 - slot)
        s = s_ref[slot]
        if bias is not None:
            s = s + jnp.concatenate([bias, bias], axis=1)
        m_old = m_ref[...]
        m_new = jnp.maximum(m_old, jnp.max(s, axis=0, keepdims=True))
        alpha = jnp.exp2(m_old - m_new)
        p = jnp.exp2(s - m_new)
        l_ref[...] = alpha * l_ref[...] + jnp.sum(p, axis=0, keepdims=True)
        start = pl.multiple_of(j * blk, blk)
        acc_ref[...] = alpha * acc_ref[...] + _dot(vt_ref[:, pl.ds(start, blk)], p.astype(BF16))
        m_ref[...] = m_new

    n_far = jnp.maximum(i - 1, 0)
    scores(0, 0)

    def pair_body(jj, carry):
        step(2 * jj, 0, None, True)
        step(2 * jj + 1, 1, None, True)
        return carry

    lax.fori_loop(0, n_far // 2, pair_body, 0)
    odd = (n_far % 2) == 1

    @pl.when(i == 0)
    def _():
        step(i, 0, bias_ref[0], False)

    @pl.when((i >= 1) & jnp.logical_not(odd))
    def _():
        step(i - 1, 0, bias_ref[1], True)
        step(i, 1, bias_ref[0], False)

    @pl.when((i >= 1) & odd)
    def _():
        step(i - 2, 0, None, True)
        step(i - 1, 1, bias_ref[1], True)
        step(i, 0, bias_ref[0], False)

    o = acc_ref[...] / l_ref[...]
    o_ref[...] = (o[:, :blk] - lam_ref[0] * o[:, blk:]).T


def _attn_prompt(lam, q, kb, vt, tiles, blk):
    t = q.shape[0]
    return pl.pallas_call(
        functools.partial(_attn_prompt_kernel, blk=blk),
        grid=(ATT_HEADS, t // blk),
        in_specs=[pl.BlockSpec(memory_space=pltpu.SMEM),
                  pl.BlockSpec((blk, LANES), lambda h, i: (i, h)),
                  pl.BlockSpec((t, LANES), lambda h, i: (0, h)),
                  pl.BlockSpec((HEAD_V, t), lambda h, i: (h, 0)),
                  pl.BlockSpec((None, 2, blk, blk), lambda h, i: (h, 0, 0, 0))],
        out_specs=pl.BlockSpec((blk, LANES), lambda h, i: (i, h)),
        out_shape=jax.ShapeDtypeStruct((t, ATT_V_W), F32),
        scratch_shapes=[pltpu.VMEM((1, 2 * blk), F32), pltpu.VMEM((1, 2 * blk), F32),
                        pltpu.VMEM((HEAD_V, 2 * blk), F32), pltpu.VMEM((2, blk, 2 * blk), F32)],
        compiler_params=_params(("parallel", "parallel")),
        name="attn_prompt",
    )(lam, q, kb, vt, tiles)


def _attn_sample_kernel(pt_ref, lam_ref, q_ref, kn_ref, vn_ref, sb_ref, *rest, pages_per_step):
    del pt_ref
    pp = pages_per_step
    page_w = ATT_HEADS * PAGE_SIZE
    k_refs = rest[:pp]
    v_refs = rest[pp:2 * pp]
    o_ref, m_ref, l_ref, acc_ref = rest[2 * pp:]
    g = pl.program_id(1)
    last = g == pl.num_programs(1) - 1
    row = lax.broadcasted_iota(I32, (2 * ATT_HEADS, LANES), 0)
    lane = lax.broadcasted_iota(I32, (2 * ATT_HEADS, LANES), 1)

    def rows8(x4):
        out = jnp.zeros((2 * ATT_HEADS, LANES), F32)
        for h in range(ATT_HEADS):
            out = jnp.where((row >> 1) == h, x4[h:h + 1, :], out)
        return out

    map_lanes = (row & 1) == (lane >= HEAD_QK).astype(I32)
    q8 = jnp.where(map_lanes, rows8(q_ref[0]) * (HEAD_QK ** -0.5 * LOG2E), 0.0)

    @pl.when(g == 0)
    def _():
        s_self = jnp.sum(q8 * rows8(kn_ref[0]), axis=-1, keepdims=True) + sb_ref[:, page_w:page_w + 1]
        m_ref[...] = s_self
        l_ref[...] = jnp.ones_like(l_ref)
        acc_ref[...] = rows8(vn_ref[0])

    qb = q8.astype(BF16)
    ss = [_dot(qb, k_refs[p][...].astype(BF16), NT) for p in range(pp)]
    ss[-1] = ss[-1] + jnp.where(last, sb_ref[:, :page_w], 0.0)
    s_all = jnp.concatenate(ss, axis=-1)
    col_head = lax.broadcasted_iota(I32, s_all.shape, 1) & (ATT_HEADS - 1)
    row_head = lax.broadcasted_iota(I32, s_all.shape, 0) >> 1
    s_all = jnp.where(col_head == row_head, s_all, -jnp.inf)
    m_old = m_ref[...]
    m_new = jnp.maximum(m_old, jnp.max(s_all, axis=-1, keepdims=True))
    alpha = jnp.exp2(m_old - m_new)
    p_all = jnp.exp2(s_all - m_new).astype(BF16)
    l_ref[...] = alpha * l_ref[...] + jnp.sum(p_all.astype(F32), axis=-1, keepdims=True)
    acc = alpha * acc_ref[...]
    for p in range(pp):
        acc = acc + _dot(p_all[:, p * page_w:(p + 1) * page_w], v_refs[p][...].astype(BF16))
    acc_ref[...] = acc
    m_ref[...] = m_new

    @pl.when(last)
    def _():
        o = acc_ref[...] / l_ref[...]
        o_ref[0] = jnp.concatenate([o[2 * h:2 * h + 1, :] - lam_ref[0] * o[2 * h + 1:2 * h + 2, :]
                                    for h in range(ATT_HEADS)], axis=0)


def _attn_sample(lam, q, k_new, v_new, cache_k, cache_v, layer, page_table, sbias):
    db, n_pages = page_table.shape
    pp = next(p for p in (16, 8, 1) if n_pages % p == 0)
    page_w = ATT_HEADS * PAGE_SIZE
    tok = pl.BlockSpec((1, ATT_HEADS, LANES), lambda b, g, pt: (b, 0, 0))

    def page_spec(p):
        return pl.BlockSpec((None, None, page_w, LANES), lambda b, g, pt: (layer, pt[b, g * pp + p], 0, 0))

    pages = lambda c: c.reshape(c.shape[0], c.shape[1], page_w, LANES)
    shp = (db, ATT_HEADS, LANES)
    out = pl.pallas_call(
        functools.partial(_attn_sample_kernel, pages_per_step=pp),
        grid_spec=pltpu.PrefetchScalarGridSpec(
            num_scalar_prefetch=1,
            grid=(db, n_pages // pp),
            in_specs=[pl.BlockSpec(memory_space=pltpu.SMEM), tok, tok, tok, _full(sbias.shape)]
                     + [page_spec(p) for p in range(pp)] * 2,
            out_specs=tok,
            scratch_shapes=[pltpu.VMEM((2 * ATT_HEADS, 1), F32), pltpu.VMEM((2 * ATT_HEADS, 1), F32),
                            pltpu.VMEM((2 * ATT_HEADS, LANES), F32)]),
        out_shape=jax.ShapeDtypeStruct(shp, F32),
        compiler_params=_params(("parallel", "arbitrary")),
        name="attn_sample",
    )(page_table, lam, q.reshape(shp), k_new.reshape(shp), v_new.reshape(shp), sbias,
      *([pages(cache_k)] * pp), *([pages(cache_v)] * pp))
    return out.reshape(db, ATT_V_W)


def _merge_kernel(x_ref, y_ref, r_ref, kx_ref, v_ref, g_ref, o_ref, gate_ref,
                  lnw_ref, lnb_ref, rk_ref, sub_ref, seg_ref, wbr_ref, wba_ref, wout_ref, nffn_ref,
                  x1_ref, xnt_ref, *, attn_scale):
    seg = seg_ref[...]
    y = y_ref[...]
    inv_n = 1.0 / RWKV_HEAD
    mean = _mm_exact(y, seg) * inv_n
    yc = y - mean
    var = _mm_exact(yc * yc, seg) * inv_n
    yn = yc * lax.rsqrt(var + RWKV_LN_EPS) * lnw_ref[...] + lnb_ref[...]
    v = v_ref[...]
    bonus = _mm_exact(r_ref[...] * kx_ref[...] * rk_ref[...], seg) * v
    y_rwkv = (yn + bonus) * g_ref[...]
    o = o_ref[...]
    parts = []
    for h in range(ATT_HEADS):
        oh = o[:, h * HEAD_V:(h + 1) * HEAD_V]
        parts.append(_rms(oh, sub_ref[...], SUBLN_EPS) * attn_scale)
    y_attn = jnp.concatenate(parts, axis=-1)
    gates = gate_ref[...]
    merged = (gates[:, :D_MODEL] * _mm(y_rwkv, wbr_ref[...])
              + gates[:, D_MODEL:] * _mm(y_attn, wba_ref[...]))
    x1 = x_ref[...] + _mm(merged, wout_ref[...])
    x1_ref[...] = x1
    xnt_ref[...] = _rms(x1, nffn_ref[...], NORM_EPS).T.astype(BF16)


def _merge(x, y, r, kx, v, g, o, gates, wts, attn_scale):
    t = x.shape[0]
    tm = min(256, t)
    row = lambda n: pl.BlockSpec((tm, n), lambda i: (i, 0))
    vec = lambda n: _full((1, n))
    return pl.pallas_call(
        functools.partial(_merge_kernel, attn_scale=attn_scale),
        grid=(t // tm,),
        in_specs=[row(D_MODEL)] + [row(RWKV_W)] * 6 + [row(2 * D_MODEL),
                  vec(RWKV_W), vec(RWKV_W), vec(RWKV_W), vec(HEAD_V), _full((RWKV_W, RWKV_W)),
                  _full((RWKV_W, D_MODEL)), _full((ATT_V_W, D_MODEL)), _full((D_MODEL, D_MODEL)),
                  vec(D_MODEL)],
        out_specs=(row(D_MODEL), pl.BlockSpec((D_MODEL, tm), lambda i: (0, i))),
        out_shape=(jax.ShapeDtypeStruct((t, D_MODEL), F32), jax.ShapeDtypeStruct((D_MODEL, t), BF16)),
        compiler_params=_params(("parallel",)),
        name="merge",
    )(x, y, r, kx, v, g, o, gates, wts['ln_w'], wts['ln_b'], wts['r_k'], wts['subln'], wts['seg64'],
      wts['w_br_rwkv'], wts['w_br_attn'], wts['w_out'], wts['norm_ffn'])


def _top_sorted(vals, k):
    rows = []
    for _ in range(k):
        mx = jnp.max(vals, axis=0, keepdims=True)
        rows.append(mx)
        vals = jnp.where(vals == mx, -jnp.inf, vals)
    return rows


def _peer_route_kernel(xnt_ref, wq_ref, k1_ref, k2_ref, rank_ref, nsel_ref, e1_ref, e2_ref):
    half = PEER_TOPK // 2
    qt = _dot(wq_ref[...], xnt_ref[...])
    k1 = k1_ref[...].astype(BF16)
    k2 = k2_ref[...].astype(BF16)
    for h in range(PEER_HEADS):
        base = h * PEER_QDIM
        s1 = _dot(k1, qt[base:base + PEER_HALF, :].astype(BF16))
        s2 = _dot(k2, qt[base + PEER_HALF:base + PEER_QDIM, :].astype(BF16))
        v1 = _top_sorted(s1, PEER_TOPK)
        v2 = _top_sorted(s2, PEER_TOPK)
        v1m = jnp.concatenate(v1, axis=0)
        v2m = jnp.concatenate(v2, axis=0)
        groups = ([v1[0] + v2m] + [v1[i] + v2m[:half] for i in range(1, half)] + [v1m[half:] + v2[0]])
        cand = jnp.concatenate(groups, axis=0)
        tau = _top_sorted(cand, PEER_TOPK)[-1]
        top = v1[0] + v2[0]
        z = jnp.sum(jnp.where(cand >= tau, jnp.exp(cand - top), 0.0), axis=0, keepdims=True)
        count = lambda grp: jnp.sum(jnp.where(grp >= tau, 1.0, 0.0), axis=0, keepdims=True)
        n_rank = [count(groups[i]) for i in range(half)]
        tail = jnp.where(groups[half] >= tau, 1.0, 0.0)
        n_rank += [tail[i:i + 1, :] for i in range(half)]
        nsel = jnp.zeros(s1.shape, F32)
        rank2 = jnp.full(s2.shape, float(PEER_TOPK), F32)
        for i in range(PEER_TOPK):
            nsel = jnp.where(s1 == v1[i], n_rank[i], nsel)
            rank2 = jnp.where(s2 == v2[i], float(i), rank2)
        rank_ref[h] = rank2.astype(BF16)
        nsel_ref[h] = nsel
        e1_ref[h] = jnp.exp(s1 - v1[0]) / z
        e2_ref[h] = jnp.exp(s2 - v2[0]).astype(BF16)


def _peer_tokens(t):
    return min(512, t)


def _peer_route(xnt, wts):
    t = xnt.shape[1]
    tb = _peer_tokens(t)
    col = pl.BlockSpec((PEER_HEADS, N_KEYS, tb), lambda i: (0, 0, i))
    big = lambda dt: jax.ShapeDtypeStruct((PEER_HEADS, N_KEYS, t), dt)
    return pl.pallas_call(
        _peer_route_kernel,
        grid=(t // tb,),
        in_specs=[pl.BlockSpec((D_MODEL, tb), lambda i: (0, i)),
                  _full((PEER_HEADS * PEER_QDIM, D_MODEL)),
                  _full((N_KEYS, PEER_HALF)), _full((N_KEYS, PEER_HALF))],
        out_specs=(col, col, col, col),
        out_shape=(big(BF16), big(F32), big(F32), big(BF16)),
        compiler_params=_params(("parallel",)),
        name="peer_route",
    )(xnt, wts['wqT'], wts['peer_k1'], wts['peer_k2'])


ROW_TILE = 16


def _peer_dense_kernel(xnt_ref, rank_ref, nsel_ref, e1_ref, e2_ref, eu_ref, evt_ref, x1_ref,
                       o_ref, yt_ref, u_ref, act_ref, gated_ref, bn_ref, be1_ref, *, eb):
    j = pl.program_id(1)

    @pl.when(j == 0)
    def _():
        yt_ref[...] = jnp.zeros_like(yt_ref)
        act_ref[...] = jnp.zeros_like(act_ref)

    xnt = xnt_ref[...]
    half = eb // 2
    for c in range(2):
        u_ref[c * half:(c + 1) * half, :] = _dot(eu_ref[c * half:(c + 1) * half, :], xnt)
    groups = eb // N_KEYS
    tb = xnt.shape[1]
    zero = jnp.zeros((ROW_TILE, tb), BF16)
    live = jnp.where(j >= 1, 1.0, 0.0)
    prev = jnp.maximum(j - 1, 0)
    for a in range(groups):
        i1 = prev * groups + a
        for h in range(PEER_HEADS):
            bn_ref[a, h] = jnp.broadcast_to(nsel_ref[h, pl.ds(i1, 1), :], (ROW_TILE, tb)).astype(BF16)
            be1_ref[a, h] = jnp.broadcast_to(e1_ref[h, pl.ds(i1, 1), :] * live, (ROW_TILE, tb)).astype(BF16)
        for rt in range(N_KEYS // ROW_TILE):
            rows = slice(rt * ROW_TILE, (rt + 1) * ROW_TILE)
            w = None
            for h in range(PEER_HEADS):
                wh = jnp.where(rank_ref[h, rows, :] < bn_ref[a, h], e2_ref[h, rows, :], zero) * be1_ref[a, h]
                w = wh if w is None else w + wh
            erows = slice(a * N_KEYS + rt * ROW_TILE, a * N_KEYS + (rt + 1) * ROW_TILE)
            gated_ref[erows, :] = w * act_ref[erows, :]
    yt_ref[...] += _dot(evt_ref[...], gated_ref[...])
    u = u_ref[...]
    act_ref[...] = (0.5 * u * (1.0 + lax.erf(u * (2.0 ** -0.5)))).astype(BF16)

    @pl.when(j == pl.num_programs(1) - 1)
    def _():
        o_ref[...] = x1_ref[...] + yt_ref[...].T


def _peer_dense(xnt, route, x1, wts):
    t = x1.shape[0]
    tb = _peer_tokens(t)
    eb = 512
    nblk = N_EXPERTS // eb
    col = pl.BlockSpec((PEER_HEADS, N_KEYS, tb), lambda i, j: (0, 0, i))
    return pl.pallas_call(
        functools.partial(_peer_dense_kernel, eb=eb),
        grid=(t // tb, nblk + 1),
        in_specs=[pl.BlockSpec((D_MODEL, tb), lambda i, j: (0, i)), col, col, col, col,
                  pl.BlockSpec((eb, D_MODEL), lambda i, j: (jnp.minimum(j, nblk - 1), 0)),
                  pl.BlockSpec((D_MODEL, eb), lambda i, j: (0, jnp.maximum(j - 1, 0))),
                  pl.BlockSpec((tb, D_MODEL), lambda i, j: (i, 0))],
        out_specs=pl.BlockSpec((tb, D_MODEL), lambda i, j: (i, 0)),
        out_shape=jax.ShapeDtypeStruct((t, D_MODEL), F32),
        scratch_shapes=[pltpu.VMEM((D_MODEL, tb), F32), pltpu.VMEM((eb, tb), F32),
                        pltpu.VMEM((eb, tb), BF16), pltpu.VMEM((eb, tb), BF16),
                        pltpu.VMEM((eb // N_KEYS, PEER_HEADS, ROW_TILE, tb), BF16),
                        pltpu.VMEM((eb // N_KEYS, PEER_HEADS, ROW_TILE, tb), BF16)],
        compiler_params=_params(("parallel", "arbitrary")),
        name="peer_dense",
    )(xnt, *route, wts['eu'], wts['evT'], x1)


def _ple_kernel(x_ref, pe_ref, wple_ref, nple_ref, wgate_ref, nfin_ref, o_ref, *, final_norm):
    x = x_ref[...]
    gate = jax.nn.sigmoid(_mm(_rms(x, nple_ref[...], NORM_EPS), wgate_ref[...]))
    x = x + _mm(pe_ref[...], wple_ref[...]) * gate
    o_ref[...] = _rms(x, nfin_ref[...], NORM_EPS) if final_norm else x


def _ple(x, pe, wts, final_norm):
    t = x.shape[0]
    tm = min(256, t)
    row = lambda n: pl.BlockSpec((tm, n), lambda i: (i, 0))
    return pl.pallas_call(
        functools.partial(_ple_kernel, final_norm=final_norm),
        grid=(t // tm,),
        in_specs=[row(D_MODEL), row(PLE_DIM), _full((PLE_DIM, D_MODEL)), _full((1, D_MODEL)),
                  _full((D_MODEL, D_MODEL)), _full((1, D_MODEL))],
        out_specs=row(D_MODEL),
        out_shape=jax.ShapeDtypeStruct((t, D_MODEL), F32),
        compiler_params=_params(("parallel",)),
        name="ple_final",
    )(x, pe, wts['w_ple'], wts['norm_ple'], wts['w_ple_gate'], wts['norm_final'])


def _token_stages(x, pe, y_rwkv_raw, pre, o_attn, gates, wts, attn_scale, final_norm):
    r, _, kx, v, _, _, g = pre
    x1, xnt = _merge(x, y_rwkv_raw, r, kx, v, g, o_attn, gates, wts, attn_scale)
    x2 = _peer_dense(xnt, _peer_route(xnt, wts), x1, wts)
    return _ple(x2, pe, wts, final_norm)


def kernel(x_prompt, x_sample, cache_k, cache_v, state_wkv, state_shift, page_table, p_prompt, p_sample,
           norm_mix, w_in, rwkv_mu, rwkv_w0, rwkv_w2, rwkv_a0, rwkv_a2, rwkv_g2, rwkv_k_k, rwkv_k_a,
           rwkv_r_k, rwkv_ln_w, rwkv_ln_b, attn_lq1, attn_lk1, attn_lq2, attn_lk2, attn_subln, rel_bias,
           w_br_rwkv, w_br_attn, w_out, norm_ffn, peer_wq, peer_k1, peer_k2, peer_u, peer_v,
           norm_ple, w_ple, w_ple_gate, norm_final):
    depth = w_in.shape[0]
    bsz, seq = x_prompt.shape[0], x_prompt.shape[1]
    db, dseq = x_sample.shape[0], x_sample.shape[1]
    assert bsz == 1 and dseq == 1, "prompt batch and decode length are fixed at 1"
    blk = next((b for b in (512, 256) if seq % b == 0), seq)
    xp = x_prompt.reshape(seq, D_MODEL)
    xs = x_sample.reshape(db, D_MODEL)
    tiles, sbias = _bias_tiles(rel_bias, blk)
    seg64 = (jnp.arange(RWKV_W)[:, None] // RWKV_HEAD == jnp.arange(RWKV_W)[None, :] // RWKV_HEAD).astype(BF16)
    row = lambda a: a.reshape(1, -1)
    outs = [[] for _ in range(8)]
    for l in range(depth):
        zpad = jnp.zeros((DECAY_LORA, RWKV_W), F32)
        wts = {
            'mu': row(rwkv_mu[l]), 'w0': row(rwkv_w0[l]), 'a0': row(rwkv_a0[l]),
            'k_k': row(rwkv_k_k[l]), 'k_a': row(rwkv_k_a[l]),
            'w2p': jnp.concatenate([rwkv_w2[l], zpad], axis=0),
            'a2p': jnp.concatenate([zpad, rwkv_a2[l]], axis=0),
            'g2': rwkv_g2[l], 'seg64': seg64,
            'ln_w': row(rwkv_ln_w[l]), 'ln_b': row(rwkv_ln_b[l]), 'r_k': row(rwkv_r_k[l]),
            'subln': row(attn_subln[l]),
            'w_br_rwkv': w_br_rwkv[l].astype(BF16), 'w_br_attn': w_br_attn[l].astype(BF16),
            'w_out': w_out[l].astype(BF16), 'norm_ffn': row(norm_ffn[l]),
            'wqT': peer_wq[l].T.astype(BF16),
            'peer_k1': peer_k1[l], 'peer_k2': peer_k2[l],
            'eu': peer_u[l].astype(BF16), 'evT': peer_v[l].T.astype(BF16),
            'w_ple': w_ple[l].astype(BF16), 'norm_ple': row(norm_ple[l]),
            'w_ple_gate': w_ple_gate[l].astype(BF16), 'norm_final': row(norm_final),
        }
        lam_init = 0.8 - 0.6 * math.exp(-0.3 * l)
        lam = (jnp.exp(jnp.sum(attn_lq1[l] * attn_lk1[l])) - jnp.exp(jnp.sum(attn_lq2[l] * attn_lk2[l]))
               + lam_init).reshape(1).astype(F32)
        w_in_b = w_in[l].astype(BF16)
        g_mix = row(norm_mix[l])

        zr, q, k, v, kb, vt, gates = _proj(xp, g_mix, w_in_b)
        pre = _rwkv_pre(zr, jnp.zeros((1, RWKV_PROJ), F32), False, wts)
        y_raw, wkv_p = _rwkv_chunked(*pre[:6])
        o = _attn_prompt(lam, q, kb, vt, tiles, blk)
        xp = _token_stages(xp, p_prompt[l].reshape(seq, PLE_DIM), y_raw, pre, o, gates, wts,
                           1.0 - lam_init, l == depth - 1)
        outs[0].append(k.reshape(1, seq, ATT_HEADS, 2 * HEAD_QK))
        outs[1].append(v.reshape(1, seq, ATT_HEADS, HEAD_V))
        outs[4].append(wkv_p[None])
        outs[6].append(zr[seq - 1:seq, :])

        zr, q, k, v, _, _, gates = _proj(xs, g_mix, w_in_b)
        pre = _rwkv_pre(zr, state_shift[l], True, wts)
        r, lw, kx, vv, kk, ka, _ = pre
        y_raw, wkv_s = _rwkv_step(state_wkv[l], r, lw, kx, kk, ka, vv)
        o = _attn_sample(lam, q, k, v, cache_k, cache_v, l, page_table, sbias)
        xs = _token_stages(xs, p_sample[l].reshape(db, PLE_DIM), y_raw, pre, o, gates, wts,
                           1.0 - lam_init, l == depth - 1)
        outs[2].append(k.reshape(db, 1, ATT_HEADS, 2 * HEAD_QK))
        outs[3].append(v.reshape(db, 1, ATT_HEADS, HEAD_V))
        outs[5].append(wkv_s)
        outs[7].append(zr)
    y_prompt = xp.reshape(1, seq, D_MODEL)
    y_sample = xs.reshape(db, 1, D_MODEL)
    st = lambda i: jnp.stack(outs[i])
    return (y_prompt, y_sample, st(0), st(1), st(2), st(3), st(4), st(5), st(6), st(7))
```

```python
import functools
import math

import jax
import jax.numpy as jnp
from jax import lax
from jax.experimental import pallas as pl
from jax.experimental.pallas import tpu as pltpu

F32 = jnp.float32
BF16 = jnp.bfloat16
I32 = jnp.int32

D_MODEL = 1024
PAGE_SIZE = 128
RWKV_HEADS = 8
RWKV_HEAD = 64
RWKV_W = RWKV_HEADS * RWKV_HEAD
DECAY_LORA = 64
AAA_LORA = 64
GATE_LORA = 128
RWKV_PROJ = 3 * RWKV_W + DECAY_LORA + AAA_LORA + GATE_LORA
RWKV_LN_EPS = 64e-5
ATT_HEADS = 4
HEAD_QK = 64
HEAD_V = 2 * HEAD_QK
ATT_QK_W = ATT_HEADS * 2 * HEAD_QK
ATT_V_W = ATT_HEADS * HEAD_V
SUBLN_EPS = 1e-5
N_BUCKETS = 32
MAX_EXACT = N_BUCKETS // 2
MAX_DISTANCE = 128
PEER_HEADS = 8
PEER_QDIM = 256
PEER_HALF = PEER_QDIM // 2
N_KEYS = 128
N_EXPERTS = N_KEYS * N_KEYS
PEER_TOPK = 16
PLE_DIM = 256
NORM_EPS = 1e-6
NEG_INF = -1e30

LANES = 128
QUAD = 4 * RWKV_HEAD
CHUNK = 64
VMEM_LIMIT = 56 * 1024 * 1024
LOG2E = math.log2(math.e)
RWKV_PASSES = 1

NN = (((1,), (0,)), ((), ()))
NT = (((1,), (1,)), ((), ()))
TN = (((0,), (0,)), ((), ()))


def _dot(a, b, dims=NN):
    return lax.dot_general(a, b, dims, preferred_element_type=F32)


def _split(x, n):
    parts = []
    for _ in range(n):
        p = x.astype(BF16)
        parts.append(p)
        x = x - p.astype(F32)
    return parts


def _mm(a, b, dims=NN, passes=1):
    if passes == 1:
        return _dot(a.astype(BF16), b.astype(BF16), dims)
    ah, al = _split(a, 2)
    bh, bl = _split(b, 2)
    return _dot(ah, bh, dims) + (_dot(ah, bl, dims) + _dot(al, bh, dims))


def _mm_exact(a, b, dims=NN):
    a1, a2, a3 = _split(a, 3)
    return _dot(a1, b, dims) + (_dot(a2, b, dims) + _dot(a3, b, dims))


def _rms(x, g, eps):
    return x * lax.rsqrt(jnp.mean(x * x, axis=-1, keepdims=True) + eps) * g


def _full(shape):
    return pl.BlockSpec(shape, lambda *_: (0,) * len(shape))


def _params(sem):
    return pltpu.CompilerParams(dimension_semantics=sem, vmem_limit_bytes=VMEM_LIMIT)


def _t5_bucket(n):
    nf = jnp.maximum(n, 1).astype(F32)
    large = MAX_EXACT + (jnp.log(nf / MAX_EXACT) / math.log(MAX_DISTANCE / MAX_EXACT)
                         * (N_BUCKETS - MAX_EXACT)).astype(I32)
    large = jnp.minimum(large, N_BUCKETS - 1)
    return jnp.where(n < MAX_EXACT, n, large)


def _bias_kernel(tab_ref, tiles_ref, sb_ref, *, blk):
    r = lax.broadcasted_iota(I32, (blk, blk), 0)
    c = lax.broadcasted_iota(I32, (blk, blk), 1)
    for t in range(2):
        dist = c - r + t * blk
        bucket = _t5_bucket(jnp.maximum(dist, 0))
        for h in range(ATT_HEADS):
            far = tab_ref[N_BUCKETS - 1, h]
            val = jnp.zeros((blk, blk), F32)
            for b in range(N_BUCKETS - 1):
                val = jnp.where(bucket == b, (tab_ref[b, h] - far) * LOG2E, val)
            if t == 0:
                val = jnp.where(dist >= 0, val, NEG_INF)
            tiles_ref[h, t] = val
    page_w = ATT_HEADS * PAGE_SIZE
    sr = lax.broadcasted_iota(I32, (2 * ATT_HEADS, page_w + LANES), 0)
    sc = lax.broadcasted_iota(I32, (2 * ATT_HEADS, page_w + LANES), 1)
    sbucket = _t5_bucket(jnp.where(sc < page_w, PAGE_SIZE - (sc >> 2), 0))
    sval = jnp.zeros(sr.shape, F32)
    for h in range(ATT_HEADS):
        far = tab_ref[N_BUCKETS - 1, h]
        hval = jnp.zeros(sr.shape, F32)
        for b in range(N_BUCKETS - 1):
            hval = jnp.where(sbucket == b, (tab_ref[b, h] - far) * LOG2E, hval)
        sval = jnp.where((sr >> 1) == h, hval, sval)
    sb_ref[...] = sval


def _bias_tiles(rel_bias, blk):
    return pl.pallas_call(
        functools.partial(_bias_kernel, blk=blk),
        out_shape=(jax.ShapeDtypeStruct((ATT_HEADS, 2, blk, blk), F32),
                   jax.ShapeDtypeStruct((2 * ATT_HEADS, ATT_HEADS * PAGE_SIZE + LANES), F32)),
        in_specs=[pl.BlockSpec(memory_space=pltpu.SMEM)],
        name="t5_bias_tiles",
    )(rel_bias)


def _proj_kernel(x_ref, g_ref, w_ref, zr_ref, q_ref, k_ref, v_ref, kb_ref, vt_ref, gate_ref):
    h = _rms(x_ref[...], g_ref[...], NORM_EPS).astype(BF16)
    o = 0
    zr_ref[...] = _dot(h, w_ref[:, o:o + RWKV_PROJ]); o += RWKV_PROJ
    q_ref[...] = _dot(h, w_ref[:, o:o + ATT_QK_W]); o += ATT_QK_W
    k = _dot(h, w_ref[:, o:o + ATT_QK_W]); o += ATT_QK_W
    v = _dot(h, w_ref[:, o:o + ATT_V_W]); o += ATT_V_W
    tm = k.shape[0]
    for hd in range(ATT_HEADS):
        k_ref[pl.ds(hd, tm, stride=ATT_HEADS), :] = k[:, hd * LANES:(hd + 1) * LANES]
        v_ref[pl.ds(hd, tm, stride=ATT_HEADS), :] = v[:, hd * LANES:(hd + 1) * LANES]
    kb_ref[...] = k.astype(BF16)
    vt_ref[...] = v.T.astype(BF16)
    gate_ref[...] = jax.nn.sigmoid(_dot(h, w_ref[:, o:o + 2 * D_MODEL]))


def _proj(x, g, w_bf16):
    t = x.shape[0]
    tm = min(256, t)
    n_total = w_bf16.shape[1]
    row = lambda n: pl.BlockSpec((tm, n), lambda i: (i, 0))
    sds = lambda n, dt=F32: jax.ShapeDtypeStruct((t, n), dt)
    heads = pl.BlockSpec((ATT_HEADS * tm, LANES), lambda i: (i, 0))
    heads_sds = jax.ShapeDtypeStruct((ATT_HEADS * t, LANES), F32)
    return pl.pallas_call(
        _proj_kernel,
        grid=(t // tm,),
        in_specs=[row(D_MODEL), _full((1, D_MODEL)), _full((D_MODEL, n_total))],
        out_specs=(row(RWKV_PROJ), row(ATT_QK_W), heads, heads,
                   row(ATT_QK_W), pl.BlockSpec((ATT_V_W, tm), lambda i: (0, i)), row(2 * D_MODEL)),
        out_shape=(sds(RWKV_PROJ), sds(ATT_QK_W), heads_sds, heads_sds,
                   sds(ATT_QK_W, BF16), jax.ShapeDtypeStruct((ATT_V_W, t), BF16), sds(2 * D_MODEL)),
        compiler_params=_params(("parallel",)),
        name="in_proj",
    )(x, g, w_bf16)


def _rwkv_pre_kernel(*refs, per_token_prev):
    if per_token_prev:
        z_ref, zp_ref = refs[:2]
        rest = refs[2:]
    else:
        z_ref, halo_ref, s0_ref = refs[:3]
        rest = refs[3:]
    (mu_ref, w0_ref, a0_ref, kk_ref, ka_ref, w2_ref, a2_ref, g2_ref, seg_ref,
     r_out, lw_out, kx_out, v_out, kk_out, kka_out, g_out) = rest
    z = z_ref[...]
    if per_token_prev:
        zp = zp_ref[...]
    else:
        first = jnp.where(pl.program_id(0) == 0, s0_ref[...], halo_ref[7:8, :])
        rowid = lax.broadcasted_iota(I32, z.shape, 0)
        zp = jnp.where(rowid == 0, first, pltpu.roll(z, 1, axis=0))
    zs = z + (zp - z) * mu_ref[...]
    r = zs[:, 0:RWKV_W]
    k = zs[:, RWKV_W:2 * RWKV_W]
    v = zs[:, 2 * RWKV_W:3 * RWKV_W]
    wa = zs[:, 3 * RWKV_W:3 * RWKV_W + DECAY_LORA + AAA_LORA]
    g_lo = zs[:, 3 * RWKV_W + DECAY_LORA + AAA_LORA:]
    wpre = w0_ref[...] + _mm(jnp.tanh(wa), w2_ref[...], passes=3)
    nw = -wpre
    softplus = jnp.maximum(nw, 0.0) + jnp.log1p(jnp.exp(-jnp.abs(nw)))
    w = -softplus - 0.5
    a = jax.nn.sigmoid(a0_ref[...] + _mm(wa, a2_ref[...], passes=3))
    g = _mm(jax.nn.sigmoid(g_lo), g2_ref[...], passes=3)
    kk = k * kk_ref[...]
    nrm = jnp.sqrt(_mm_exact(kk * kk, seg_ref[...]))
    kk = kk / jnp.maximum(nrm, 1e-12)
    r_out[...] = r
    lw_out[...] = -jnp.exp(w)
    kx_out[...] = k * (1.0 + (a - 1.0) * ka_ref[...])
    v_out[...] = v
    kk_out[...] = kk
    kka_out[...] = kk * a
    g_out[...] = g


def _rwkv_pre(z, zprev_or_shift0, per_token_prev, wts):
    t = z.shape[0]
    tm = min(256, t)
    row = lambda n: pl.BlockSpec((tm, n), lambda i: (i, 0))
    vec = lambda n: _full((1, n))
    if per_token_prev:
        lead = [row(RWKV_PROJ), row(RWKV_PROJ)]
        args = [z, zprev_or_shift0]
    else:
        halo = pl.BlockSpec((8, RWKV_PROJ), lambda i: (jnp.maximum(i * (tm // 8) - 1, 0), 0))
        lead = [row(RWKV_PROJ), halo, vec(RWKV_PROJ)]
        args = [z, z, zprev_or_shift0]
    lora_w = DECAY_LORA + AAA_LORA
    return pl.pallas_call(
        functools.partial(_rwkv_pre_kernel, per_token_prev=per_token_prev),
        grid=(t // tm,),
        in_specs=lead + [vec(RWKV_PROJ), vec(RWKV_W), vec(RWKV_W), vec(RWKV_W), vec(RWKV_W),
                         _full((lora_w, RWKV_W)), _full((lora_w, RWKV_W)), _full((GATE_LORA, RWKV_W)),
                         _full((RWKV_W, RWKV_W))],
        out_specs=tuple(row(RWKV_W) for _ in range(7)),
        out_shape=tuple(jax.ShapeDtypeStruct((t, RWKV_W), F32) for _ in range(7)),
        compiler_params=_params(("parallel",)),
        name="rwkv_pre",
    )(*args, wts['mu'], wts['w0'], wts['a0'], wts['k_k'], wts['k_a'], wts['w2p'], wts['a2p'], wts['g2'],
      wts['seg64'])


def _stack(x, lane_head):
    return jnp.concatenate([jnp.where(lane_head == h, x, 0.0) for h in range(4)], axis=0)


def _parts(x, passes):
    return tuple(_split(x, 2)) if passes == 3 else (x.astype(BF16),)


def _mmp(pa, pb, dims=NN):
    if len(pa) == 1:
        return _dot(pa[0], pb[0], dims)
    return _dot(pa[0], pb[0], dims) + (_dot(pa[0], pb[1], dims) + _dot(pa[1], pb[0], dims))


def _rwkv_chunk_kernel(r_ref, lw_ref, kx_ref, v_ref, kk_ref, ka_ref, tri_ref, y_ref, sout_ref, s_scr,
                       *, nsub, passes):
    @pl.when(pl.program_id(0) == 0)
    def _():
        s_scr[...] = jnp.zeros_like(s_scr)

    L = CHUNK
    n4 = 4 * L
    nquad = RWKV_W // QUAD
    tri = tri_ref[...]
    lane_head = lax.broadcasted_iota(I32, (L, QUAD), 1) // RWKV_HEAD
    rr = lax.broadcasted_iota(I32, (n4, n4), 0)
    cc = lax.broadcasted_iota(I32, (n4, n4), 1)
    strict = (rr % L) > (cc % L)
    incl = (rr % L) >= (cc % L)
    eye = jnp.where(rr == cc, 1.0, 0.0)
    parts = functools.partial(_parts, passes=passes)
    insts = [(c, qd) for c in range(nsub) for qd in range(nquad)]

    st = {}
    for c, qd in insts:
        rows = slice(c * L, (c + 1) * L)
        cols = slice(qd * QUAD, (qd + 1) * QUAD)
        lw, ka, kx = lw_ref[rows, cols], ka_ref[rows, cols], kx_ref[rows, cols]
        cs = _mm_exact(tri, lw)
        cs_end = cs[L - 1:L, :]
        e_neg = jnp.exp(-cs)
        e_end = jnp.exp(cs_end - cs)
        stk = lambda x: parts(_stack(x, lane_head))
        p_a = stk(-kk_ref[rows, cols] * jnp.exp(cs - lw))
        p_b = stk(ka * e_neg)
        p_k = stk(kx * e_neg)
        p_r = stk(r_ref[rows, cols] * jnp.exp(cs))
        p_v = stk(v_ref[rows, cols])
        st[c, qd] = dict(
            p_a=p_a, p_r=p_r, p_v=p_v, p_bh=stk(ka * e_end), p_kh=stk(kx * e_end), decay=jnp.exp(cs_end),
            n_ab=jnp.where(strict, _mmp(p_a, p_b, NT), 0.0), n_ak=jnp.where(strict, _mmp(p_a, p_k, NT), 0.0),
            n_rb=jnp.where(incl, _mmp(p_r, p_b, NT), 0.0), n_rk=jnp.where(incl, _mmp(p_r, p_k, NT), 0.0))

    xs = {k: d['n_ab'] for k, d in st.items()}
    tinv = {k: eye + d['n_ab'] for k, d in st.items()}
    for _ in range(5):
        for k in insts:
            px = parts(xs[k])
            xs[k] = _mmp(px, px)
            tinv[k] = tinv[k] + _mmp(parts(tinv[k]), parts(xs[k]))

    for k in insts:
        d = st[k]
        p_t = parts(tinv[k])
        d['p_wa'] = parts(_mmp(p_t, d['p_a']))
        d['xu'] = _mmp(p_t, parts(_mmp(parts(d['n_ak']), d['p_v'])))
        d['yk'] = _mmp(parts(d['n_rk']), d['p_v'])
        d['hk'] = _mmp(d['p_v'], d['p_kh'], TN)
        d['p_rb'] = parts(d['n_rb'])

    for qd in range(nquad):
        s = s_scr[qd]
        for c in range(nsub):
            d = st[c, qd]
            p_s = parts(s)
            p_u = parts(_mmp(d['p_wa'], p_s, NT) + d['xu'])
            s_y = _mmp(d['p_r'], p_s, NT) + _mmp(d['p_rb'], p_u) + d['yk']
            y = jnp.zeros((L, QUAD), F32)
            for h in range(4):
                y = y + jnp.where(lane_head == h, s_y[h * L:(h + 1) * L, :], 0.0)
            y_ref[c * L:(c + 1) * L, qd * QUAD:(qd + 1) * QUAD] = y
            s = s * d['decay'] + _mmp(p_u, d['p_bh'], TN) + d['hk']
        s_scr[qd] = s

    @pl.when(pl.program_id(0) == pl.num_programs(0) - 1)
    def _():
        sout_ref[...] = s_scr[...]


def _rwkv_chunked(r, lw, kx, v, kk, ka, passes=RWKV_PASSES):
    t = r.shape[0]
    nsub = next(n for n in (2, 1) if t % (n * CHUNK) == 0)
    lb = nsub * CHUNK
    nq = RWKV_W // QUAD
    tri = (jnp.arange(CHUNK)[:, None] >= jnp.arange(CHUNK)[None, :]).astype(BF16)
    row = pl.BlockSpec((lb, RWKV_W), lambda i: (i, 0))
    y, s_bd = pl.pallas_call(
        functools.partial(_rwkv_chunk_kernel, nsub=nsub, passes=passes),
        grid=(t // lb,),
        in_specs=[row] * 6 + [_full((CHUNK, CHUNK))],
        out_specs=(row, _full((nq, QUAD, QUAD))),
        out_shape=(jax.ShapeDtypeStruct((t, RWKV_W), F32), jax.ShapeDtypeStruct((nq, QUAD, QUAD), F32)),
        scratch_shapes=[pltpu.VMEM((nq, QUAD, QUAD), F32)],
        compiler_params=_params(("arbitrary",)),
        name="rwkv_chunked",
    )(r, lw, kx, v, kk, ka, tri)
    blocks = [s_bd[h // 4, (h % 4) * RWKV_HEAD:(h % 4 + 1) * RWKV_HEAD, (h % 4) * RWKV_HEAD:(h % 4 + 1) * RWKV_HEAD]
              for h in range(RWKV_HEADS)]
    return y, jnp.stack(blocks)


def _rwkv_step_kernel(s_ref, r_ref, lw_ref, kx_ref, kk_ref, ka_ref, v_ref, sout_ref, y_ref):
    s = s_ref[...]
    sa = -jnp.sum(s * kk_ref[...], axis=-1, keepdims=True)
    s_new = s * jnp.exp(lw_ref[...]) + sa * ka_ref[...] + v_ref[...] * kx_ref[...]
    sout_ref[...] = s_new
    y_ref[...] = jnp.sum(s_new * r_ref[...], axis=-1, keepdims=True)


def _rwkv_step(state, r, lw, kx, kk, ka, v):
    db = state.shape[0]
    bb = 8 if db % 8 == 0 else 1
    rowv = lambda a: a.reshape(db, RWKV_HEADS, 1, RWKV_HEAD)
    st = pl.BlockSpec((bb, RWKV_HEADS, RWKV_HEAD, RWKV_HEAD), lambda i: (i, 0, 0, 0))
    rv = pl.BlockSpec((bb, RWKV_HEADS, 1, RWKV_HEAD), lambda i: (i, 0, 0, 0))
    cv = pl.BlockSpec((bb, RWKV_HEADS, RWKV_HEAD, 1), lambda i: (i, 0, 0, 0))
    s_new, y = pl.pallas_call(
        _rwkv_step_kernel,
        grid=(db // bb,),
        in_specs=[st, rv, rv, rv, rv, rv, cv],
        out_specs=(st, cv),
        out_shape=(jax.ShapeDtypeStruct(state.shape, F32),
                   jax.ShapeDtypeStruct((db, RWKV_HEADS, RWKV_HEAD, 1), F32)),
        compiler_params=_params(("parallel",)),
        name="rwkv_step",
    )(state, rowv(r), rowv(lw), rowv(kx), rowv(kk), rowv(ka), v.reshape(db, RWKV_HEADS, RWKV_HEAD, 1))
    return y.reshape(db, RWKV_W), s_new


def _attn_prompt_kernel(lam_ref, q_ref, k_ref, vt_ref, bias_ref, o_ref, m_ref, l_ref, acc_ref, s_ref, *, blk):
    i = pl.program_id(1)
    qt = q_ref[...].T * (HEAD_QK ** -0.5 * LOG2E)
    sub = lax.broadcasted_iota(I32, qt.shape, 0)
    qt2 = jnp.concatenate([jnp.where(sub < HEAD_QK, qt, 0.0), jnp.where(sub >= HEAD_QK, qt, 0.0)],
                          axis=1).astype(BF16)
    m_ref[...] = jnp.full_like(m_ref, -jnp.inf)
    l_ref[...] = jnp.zeros_like(l_ref)
    acc_ref[...] = jnp.zeros_like(acc_ref)

    def scores(j, slot):
        start = pl.multiple_of(j * blk, blk)
        s_ref[slot] = _dot(k_ref[pl.ds(start, blk), :], qt2)

    def step(j, slot, bias, prefetch):
        if prefetch:
            scores(j + 1, 1 - slot)
        s = s_ref[slot]
        if bias is not None:
            s = s + jnp.concatenate([bias, bias], axis=1)
        m_old = m_ref[...]
        m_new = jnp.maximum(m_old, jnp.max(s, axis=0, keepdims=True))
        alpha = jnp.exp2(m_old - m_new)
        p = jnp.exp2(s - m_new)
        l_ref[...] = alpha * l_ref[...] + jnp.sum(p, axis=0, keepdims=True)
        start = pl.multiple_of(j * blk, blk)
        acc_ref[...] = alpha * acc_ref[...] + _dot(vt_ref[:, pl.ds(start, blk)], p.astype(BF16))
        m_ref[...] = m_new

    n_far = jnp.maximum(i - 1, 0)
    scores(0, 0)

    def pair_body(jj, carry):
        step(2 * jj, 0, None, True)
        step(2 * jj + 1, 1, None, True)
        return carry

    lax.fori_loop(0, n_far // 2, pair_body, 0)
    odd = (n_far % 2) == 1

    @pl.when(i == 0)
    def _():
        step(i, 0, bias_ref[0], False)

    @pl.when((i >= 1) & jnp.logical_not(odd))
    def _():
        step(i - 1, 0, bias_ref[1], True)
        step(i, 1, bias_ref[0], False)

    @pl.when((i >= 1) & odd)
    def _():
        step(i - 2, 0, None, True)
        step(i - 1, 1, bias_ref[1], True)
        step(i, 0, bias_ref[0], False)

    o = acc_ref[...] / l_ref[...]
    o_ref[...] = (o[:, :blk] - lam_ref[0] * o[:, blk:]).T


def _attn_prompt(lam, q, kb, vt, tiles, blk):
    t = q.shape[0]
    return pl.pallas_call(
        functools.partial(_attn_prompt_kernel, blk=blk),
        grid=(ATT_HEADS, t // blk),
        in_specs=[pl.BlockSpec(memory_space=pltpu.SMEM),
                  pl.BlockSpec((blk, LANES), lambda h, i: (i, h)),
                  pl.BlockSpec((t, LANES), lambda h, i: (0, h)),
                  pl.BlockSpec((HEAD_V, t), lambda h, i: (h, 0)),
                  pl.BlockSpec((None, 2, blk, blk), lambda h, i: (h, 0, 0, 0))],
        out_specs=pl.BlockSpec((blk, LANES), lambda h, i: (i, h)),
        out_shape=jax.ShapeDtypeStruct((t, ATT_V_W), F32),
        scratch_shapes=[pltpu.VMEM((1, 2 * blk), F32), pltpu.VMEM((1, 2 * blk), F32),
                        pltpu.VMEM((HEAD_V, 2 * blk), F32), pltpu.VMEM((2, blk, 2 * blk), F32)],
        compiler_params=_params(("parallel", "parallel")),
        name="attn_prompt",
    )(lam, q, kb, vt, tiles)


def _attn_sample_kernel(pt_ref, lam_ref, q_ref, kn_ref, vn_ref, sb_ref, *rest, pages_per_step):
    del pt_ref
    pp = pages_per_step
    page_w = ATT_HEADS * PAGE_SIZE
    k_refs = rest[:pp]
    v_refs = rest[pp:2 * pp]
    o_ref, m_ref, l_ref, acc_ref = rest[2 * pp:]
    g = pl.program_id(1)
    last = g == pl.num_programs(1) - 1
    row = lax.broadcasted_iota(I32, (2 * ATT_HEADS, LANES), 0)
    lane = lax.broadcasted_iota(I32, (2 * ATT_HEADS, LANES), 1)

    def rows8(x4):
        out = jnp.zeros((2 * ATT_HEADS, LANES), F32)
        for h in range(ATT_HEADS):
            out = jnp.where((row >> 1) == h, x4[h:h + 1, :], out)
        return out

    map_lanes = (row & 1) == (lane >= HEAD_QK).astype(I32)
    q8 = jnp.where(map_lanes, rows8(q_ref[0]) * (HEAD_QK ** -0.5 * LOG2E), 0.0)

    @pl.when(g == 0)
    def _():
        s_self = jnp.sum(q8 * rows8(kn_ref[0]), axis=-1, keepdims=True) + sb_ref[:, page_w:page_w + 1]
        m_ref[...] = s_self
        l_ref[...] = jnp.ones_like(l_ref)
        acc_ref[...] = rows8(vn_ref[0])

    qb = q8.astype(BF16)
    ss = [_dot(qb, k_refs[p][...].astype(BF16), NT) for p in range(pp)]
    ss[-1] = ss[-1] + jnp.where(last, sb_ref[:, :page_w], 0.0)
    s_all = jnp.concatenate(ss, axis=-1)
    col_head = lax.broadcasted_iota(I32, s_all.shape, 1) & (ATT_HEADS - 1)
    row_head = lax.broadcasted_iota(I32, s_all.shape, 0) >> 1
    s_all = jnp.where(col_head == row_head, s_all, -jnp.inf)
    m_old = m_ref[...]
    m_new = jnp.maximum(m_old, jnp.max(s_all, axis=-1, keepdims=True))
    alpha = jnp.exp2(m_old - m_new)
    p_all = jnp.exp2(s_all - m_new).astype(BF16)
    l_ref[...] = alpha * l_ref[...] + jnp.sum(p_all.astype(F32), axis=-1, keepdims=True)
    acc = alpha * acc_ref[...]
    for p in range(pp):
        acc = acc + _dot(p_all[:, p * page_w:(p + 1) * page_w], v_refs[p][...].astype(BF16))
    acc_ref[...] = acc
    m_ref[...] = m_new

    @pl.when(last)
    def _():
        o = acc_ref[...] / l_ref[...]
        o_ref[0] = jnp.concatenate([o[2 * h:2 * h + 1, :] - lam_ref[0] * o[2 * h + 1:2 * h + 2, :]
                                    for h in range(ATT_HEADS)], axis=0)


def _attn_sample(lam, q, k_new, v_new, cache_k, cache_v, layer, page_table, sbias):
    db, n_pages = page_table.shape
    pp = next(p for p in (16, 8, 1) if n_pages % p == 0)
    page_w = ATT_HEADS * PAGE_SIZE
    tok = pl.BlockSpec((1, ATT_HEADS, LANES), lambda b, g, pt: (b, 0, 0))

    def page_spec(p):
        return pl.BlockSpec((None, None, page_w, LANES), lambda b, g, pt: (layer, pt[b, g * pp + p], 0, 0))

    pages = lambda c: c.reshape(c.shape[0], c.shape[1], page_w, LANES)
    shp = (db, ATT_HEADS, LANES)
    out = pl.pallas_call(
        functools.partial(_attn_sample_kernel, pages_per_step=pp),
        grid_spec=pltpu.PrefetchScalarGridSpec(
            num_scalar_prefetch=1,
            grid=(db, n_pages // pp),
            in_specs=[pl.BlockSpec(memory_space=pltpu.SMEM), tok, tok, tok, _full(sbias.shape)]
                     + [page_spec(p) for p in range(pp)] * 2,
            out_specs=tok,
            scratch_shapes=[pltpu.VMEM((2 * ATT_HEADS, 1), F32), pltpu.VMEM((2 * ATT_HEADS, 1), F32),
                            pltpu.VMEM((2 * ATT_HEADS, LANES), F32)]),
        out_shape=jax.ShapeDtypeStruct(shp, F32),
        compiler_params=_params(("parallel", "arbitrary")),
        name="attn_sample",
    )(page_table, lam, q.reshape(shp), k_new.reshape(shp), v_new.reshape(shp), sbias,
      *([pages(cache_k)] * pp), *([pages(cache_v)] * pp))
    return out.reshape(db, ATT_V_W)


def _merge_kernel(x_ref, y_ref, r_ref, kx_ref, v_ref, g_ref, o_ref, gate_ref,
                  lnw_ref, lnb_ref, rk_ref, sub_ref, seg_ref, wbr_ref, wba_ref, wout_ref, nffn_ref,
                  x1_ref, xnt_ref, *, attn_scale):
    seg = seg_ref[...]
    y = y_ref[...]
    inv_n = 1.0 / RWKV_HEAD
    mean = _mm_exact(y, seg) * inv_n
    yc = y - mean
    var = _mm_exact(yc * yc, seg) * inv_n
    yn = yc * lax.rsqrt(var + RWKV_LN_EPS) * lnw_ref[...] + lnb_ref[...]
    v = v_ref[...]
    bonus = _mm_exact(r_ref[...] * kx_ref[...] * rk_ref[...], seg) * v
    y_rwkv = (yn + bonus) * g_ref[...]
    o = o_ref[...]
    parts = []
    for h in range(ATT_HEADS):
        oh = o[:, h * HEAD_V:(h + 1) * HEAD_V]
        parts.append(_rms(oh, sub_ref[...], SUBLN_EPS) * attn_scale)
    y_attn = jnp.concatenate(parts, axis=-1)
    gates = gate_ref[...]
    merged = (gates[:, :D_MODEL] * _mm(y_rwkv, wbr_ref[...])
              + gates[:, D_MODEL:] * _mm(y_attn, wba_ref[...]))
    x1 = x_ref[...] + _mm(merged, wout_ref[...])
    x1_ref[...] = x1
    xnt_ref[...] = _rms(x1, nffn_ref[...], NORM_EPS).T.astype(BF16)


def _merge(x, y, r, kx, v, g, o, gates, wts, attn_scale):
    t = x.shape[0]
    tm = min(256, t)
    row = lambda n: pl.BlockSpec((tm, n), lambda i: (i, 0))
    vec = lambda n: _full((1, n))
    return pl.pallas_call(
        functools.partial(_merge_kernel, attn_scale=attn_scale),
        grid=(t // tm,),
        in_specs=[row(D_MODEL)] + [row(RWKV_W)] * 6 + [row(2 * D_MODEL),
                  vec(RWKV_W), vec(RWKV_W), vec(RWKV_W), vec(HEAD_V), _full((RWKV_W, RWKV_W)),
                  _full((RWKV_W, D_MODEL)), _full((ATT_V_W, D_MODEL)), _full((D_MODEL, D_MODEL)),
                  vec(D_MODEL)],
        out_specs=(row(D_MODEL), pl.BlockSpec((D_MODEL, tm), lambda i: (0, i))),
        out_shape=(jax.ShapeDtypeStruct((t, D_MODEL), F32), jax.ShapeDtypeStruct((D_MODEL, t), BF16)),
        compiler_params=_params(("parallel",)),
        name="merge",
    )(x, y, r, kx, v, g, o, gates, wts['ln_w'], wts['ln_b'], wts['r_k'], wts['subln'], wts['seg64'],
      wts['w_br_rwkv'], wts['w_br_attn'], wts['w_out'], wts['norm_ffn'])


def _top_sorted(vals, k):
    rows = []
    for _ in range(k):
        mx = jnp.max(vals, axis=0, keepdims=True)
        rows.append(mx)
        vals = jnp.where(vals == mx, -jnp.inf, vals)
    return rows


def _peer_route_kernel(xnt_ref, wq_ref, k1_ref, k2_ref, rank_ref, nsel_ref, e1_ref, e2_ref):
    half = PEER_TOPK // 2
    qt = _dot(wq_ref[...], xnt_ref[...])
    k1 = k1_ref[...].astype(BF16)
    k2 = k2_ref[...].astype(BF16)
    for h in range(PEER_HEADS):
        base = h * PEER_QDIM
        s1 = _dot(k1, qt[base:base + PEER_HALF, :].astype(BF16))
        s2 = _dot(k2, qt[base + PEER_HALF:base + PEER_QDIM, :].astype(BF16))
        v1 = _top_sorted(s1, PEER_TOPK)
        v2 = _top_sorted(s2, PEER_TOPK)
        v1m = jnp.concatenate(v1, axis=0)
        v2m = jnp.concatenate(v2, axis=0)
        groups = ([v1[0] + v2m] + [v1[i] + v2m[:half] for i in range(1, half)] + [v1m[half:] + v2[0]])
        cand = jnp.concatenate(groups, axis=0)
        tau = _top_sorted(cand, PEER_TOPK)[-1]
        top = v1[0] + v2[0]
        z = jnp.sum(jnp.where(cand >= tau, jnp.exp(cand - top), 0.0), axis=0, keepdims=True)
        count = lambda grp: jnp.sum(jnp.where(grp >= tau, 1.0, 0.0), axis=0, keepdims=True)
        n_rank = [count(groups[i]) for i in range(half)]
        tail = jnp.where(groups[half] >= tau, 1.0, 0.0)
        n_rank += [tail[i:i + 1, :] for i in range(half)]
        nsel = jnp.zeros(s1.shape, F32)
        rank2 = jnp.full(s2.shape, float(PEER_TOPK), F32)
        for i in range(PEER_TOPK):
            nsel = jnp.where(s1 == v1[i], n_rank[i], nsel)
            rank2 = jnp.where(s2 == v2[i], float(i), rank2)
        rank_ref[h] = rank2.astype(BF16)
        nsel_ref[h] = nsel
        e1_ref[h] = jnp.exp(s1 - v1[0]) / z
        e2_ref[h] = jnp.exp(s2 - v2[0]).astype(BF16)


def _peer_tokens(t):
    return min(512, t)


def _peer_route(xnt, wts):
    t = xnt.shape[1]
    tb = _peer_tokens(t)
    col = pl.BlockSpec((PEER_HEADS, N_KEYS, tb), lambda i: (0, 0, i))
    big = lambda dt: jax.ShapeDtypeStruct((PEER_HEADS, N_KEYS, t), dt)
    return pl.pallas_call(
        _peer_route_kernel,
        grid=(t // tb,),
        in_specs=[pl.BlockSpec((D_MODEL, tb), lambda i: (0, i)),
                  _full((PEER_HEADS * PEER_QDIM, D_MODEL)),
                  _full((N_KEYS, PEER_HALF)), _full((N_KEYS, PEER_HALF))],
        out_specs=(col, col, col, col),
        out_shape=(big(BF16), big(F32), big(F32), big(BF16)),
        compiler_params=_params(("parallel",)),
        name="peer_route",
    )(xnt, wts['wqT'], wts['peer_k1'], wts['peer_k2'])


ROW_TILE = 16
PEER_EB = 512


def _peer_dense_kernel(xnt_ref, rank_ref, nsel_ref, e1_ref, e2_ref, eu_ref, evt_ref, x1_ref,
                       pe_ref, wple_ref, nple_ref, wgate_ref, nfin_ref,
                       o_ref, yt_ref, u_ref, act_ref, gated_ref, bn_ref, be1_ref, *, eb, final_norm):
    j = pl.program_id(1)

    @pl.when(j == 0)
    def _():
        yt_ref[...] = jnp.zeros_like(yt_ref)
        act_ref[...] = jnp.zeros_like(act_ref)

    xnt = xnt_ref[...]
    half = eb // 2
    for c in range(2):
        u_ref[c * half:(c + 1) * half, :] = _dot(eu_ref[c * half:(c + 1) * half, :], xnt)
    groups = eb // N_KEYS
    tb = xnt.shape[1]
    zero = jnp.zeros((ROW_TILE, tb), BF16)
    live = jnp.where(j >= 1, 1.0, 0.0)
    prev = jnp.maximum(j - 1, 0)
    for a in range(groups):
        i1 = prev * groups + a
        for h in range(PEER_HEADS):
            bn_ref[a, h] = jnp.broadcast_to(nsel_ref[h, pl.ds(i1, 1), :], (ROW_TILE, tb)).astype(BF16)
            be1_ref[a, h] = jnp.broadcast_to(e1_ref[h, pl.ds(i1, 1), :] * live, (ROW_TILE, tb)).astype(BF16)
        for rt in range(N_KEYS // ROW_TILE):
            rows = slice(rt * ROW_TILE, (rt + 1) * ROW_TILE)
            w = None
            for h in range(PEER_HEADS):
                wh = jnp.where(rank_ref[h, rows, :] < bn_ref[a, h], e2_ref[h, rows, :], zero) * be1_ref[a, h]
                w = wh if w is None else w + wh
            erows = slice(a * N_KEYS + rt * ROW_TILE, a * N_KEYS + (rt + 1) * ROW_TILE)
            gated_ref[erows, :] = w * act_ref[erows, :]
    yt_ref[...] += _dot(evt_ref[...], gated_ref[...])
    u = u_ref[...]
    act_ref[...] = (0.5 * u * (1.0 + lax.erf(u * (2.0 ** -0.5)))).astype(BF16)

    @pl.when(j == pl.num_programs(1) - 1)
    def _():
        x = x1_ref[...] + yt_ref[...].T
        gate = jax.nn.sigmoid(_mm(_rms(x, nple_ref[...], NORM_EPS), wgate_ref[...]))
        x = x + _mm(pe_ref[...], wple_ref[...]) * gate
        o_ref[...] = _rms(x, nfin_ref[...], NORM_EPS) if final_norm else x


def _peer_dense(xnt, route, x1, pe, wts, final_norm):
    t = x1.shape[0]
    tb = _peer_tokens(t)
    eb = PEER_EB
    nblk = N_EXPERTS // eb
    col = pl.BlockSpec((PEER_HEADS, N_KEYS, tb), lambda i, j: (0, 0, i))
    return pl.pallas_call(
        functools.partial(_peer_dense_kernel, eb=eb, final_norm=final_norm),
        grid=(t // tb, nblk + 1),
        in_specs=[pl.BlockSpec((D_MODEL, tb), lambda i, j: (0, i)), col, col, col, col,
                  pl.BlockSpec((eb, D_MODEL), lambda i, j: (jnp.minimum(j, nblk - 1), 0)),
                  pl.BlockSpec((None, D_MODEL, eb), lambda i, j: (jnp.maximum(j - 1, 0), 0, 0)),
                  pl.BlockSpec((tb, D_MODEL), lambda i, j: (i, 0)),
                  pl.BlockSpec((tb, PLE_DIM), lambda i, j: (i, 0)),
                  _full((PLE_DIM, D_MODEL)), _full((1, D_MODEL)), _full((D_MODEL, D_MODEL)),
                  _full((1, D_MODEL))],
        out_specs=pl.BlockSpec((tb, D_MODEL), lambda i, j: (i, 0)),
        out_shape=jax.ShapeDtypeStruct((t, D_MODEL), F32),
        scratch_shapes=[pltpu.VMEM((D_MODEL, tb), F32), pltpu.VMEM((eb, tb), F32),
                        pltpu.VMEM((eb, tb), BF16), pltpu.VMEM((eb, tb), BF16),
                        pltpu.VMEM((eb // N_KEYS, PEER_HEADS, ROW_TILE, tb), BF16),
                        pltpu.VMEM((eb // N_KEYS, PEER_HEADS, ROW_TILE, tb), BF16)],
        compiler_params=_params(("parallel", "arbitrary")),
        name="peer_dense",
    )(xnt, *route, wts['eu'], wts['evT'], x1, pe,
      wts['w_ple'], wts['norm_ple'], wts['w_ple_gate'], wts['norm_final'])


def _token_stages(x, pe, y_rwkv_raw, pre, o_attn, gates, wts, attn_scale, final_norm):
    r, _, kx, v, _, _, g = pre
    x1, xnt = _merge(x, y_rwkv_raw, r, kx, v, g, o_attn, gates, wts, attn_scale)
    return _peer_dense(xnt, _peer_route(xnt, wts), x1, pe, wts, final_norm)


def kernel(x_prompt, x_sample, cache_k, cache_v, state_wkv, state_shift, page_table, p_prompt, p_sample,
           norm_mix, w_in, rwkv_mu, rwkv_w0, rwkv_w2, rwkv_a0, rwkv_a2, rwkv_g2, rwkv_k_k, rwkv_k_a,
           rwkv_r_k, rwkv_ln_w, rwkv_ln_b, attn_lq1, attn_lk1, attn_lq2, attn_lk2, attn_subln, rel_bias,
           w_br_rwkv, w_br_attn, w_out, norm_ffn, peer_wq, peer_k1, peer_k2, peer_u, peer_v,
           norm_ple, w_ple, w_ple_gate, norm_final):
    depth = w_in.shape[0]
    bsz, seq = x_prompt.shape[0], x_prompt.shape[1]
    db, dseq = x_sample.shape[0], x_sample.shape[1]
    assert bsz == 1 and dseq == 1, "prompt batch and decode length are fixed at 1"
    blk = next((b for b in (512, 256) if seq % b == 0), seq)
    xp = x_prompt.reshape(seq, D_MODEL)
    xs = x_sample.reshape(db, D_MODEL)
    tiles, sbias = _bias_tiles(rel_bias, blk)
    seg64 = (jnp.arange(RWKV_W)[:, None] // RWKV_HEAD == jnp.arange(RWKV_W)[None, :] // RWKV_HEAD).astype(BF16)
    row = lambda a: a.reshape(1, -1)
    outs = [[] for _ in range(8)]
    for l in range(depth):
        zpad = jnp.zeros((DECAY_LORA, RWKV_W), F32)
        wts = {
            'mu': row(rwkv_mu[l]), 'w0': row(rwkv_w0[l]), 'a0': row(rwkv_a0[l]),
            'k_k': row(rwkv_k_k[l]), 'k_a': row(rwkv_k_a[l]),
            'w2p': jnp.concatenate([rwkv_w2[l], zpad], axis=0),
            'a2p': jnp.concatenate([zpad, rwkv_a2[l]], axis=0),
            'g2': rwkv_g2[l], 'seg64': seg64,
            'ln_w': row(rwkv_ln_w[l]), 'ln_b': row(rwkv_ln_b[l]), 'r_k': row(rwkv_r_k[l]),
            'subln': row(attn_subln[l]),
            'w_br_rwkv': w_br_rwkv[l].astype(BF16), 'w_br_attn': w_br_attn[l].astype(BF16),
            'w_out': w_out[l].astype(BF16), 'norm_ffn': row(norm_ffn[l]),
            'wqT': peer_wq[l].T.astype(BF16),
            'peer_k1': peer_k1[l], 'peer_k2': peer_k2[l],
            'eu': peer_u[l].astype(BF16),
            'evT': peer_v[l].reshape(N_EXPERTS // PEER_EB, PEER_EB, D_MODEL).transpose(0, 2, 1).astype(BF16),
            'w_ple': w_ple[l].astype(BF16), 'norm_ple': row(norm_ple[l]),
            'w_ple_gate': w_ple_gate[l].astype(BF16), 'norm_final': row(norm_final),
        }
        lam_init = 0.8 - 0.6 * math.exp(-0.3 * l)
        lam = (jnp.exp(jnp.sum(attn_lq1[l] * attn_lk1[l])) - jnp.exp(jnp.sum(attn_lq2[l] * attn_lk2[l]))
               + lam_init).reshape(1).astype(F32)
        w_in_b = w_in[l].astype(BF16)
        g_mix = row(norm_mix[l])

        zr, q, k, v, kb, vt, gates = _proj(xp, g_mix, w_in_b)
        pre = _rwkv_pre(zr, jnp.zeros((1, RWKV_PROJ), F32), False, wts)
        y_raw, wkv_p = _rwkv_chunked(*pre[:6])
        o = _attn_prompt(lam, q, kb, vt, tiles, blk)
        xp = _token_stages(xp, p_prompt[l].reshape(seq, PLE_DIM), y_raw, pre, o, gates, wts,
                           1.0 - lam_init, l == depth - 1)
        outs[0].append(k.reshape(1, seq, ATT_HEADS, 2 * HEAD_QK))
        outs[1].append(v.reshape(1, seq, ATT_HEADS, HEAD_V))
        outs[4].append(wkv_p[None])
        outs[6].append(zr[seq - 1:seq, :])

        zr, q, k, v, _, _, gates = _proj(xs, g_mix, w_in_b)
        pre = _rwkv_pre(zr, state_shift[l], True, wts)
        r, lw, kx, vv, kk, ka, _ = pre
        y_raw, wkv_s = _rwkv_step(state_wkv[l], r, lw, kx, kk, ka, vv)
        o = _attn_sample(lam, q, k, v, cache_k, cache_v, l, page_table, sbias)
        xs = _token_stages(xs, p_sample[l].reshape(db, PLE_DIM), y_raw, pre, o, gates, wts,
                           1.0 - lam_init, l == depth - 1)
        outs[2].append(k.reshape(db, 1, ATT_HEADS, 2 * HEAD_QK))
        outs[3].append(v.reshape(db, 1, ATT_HEADS, HEAD_V))
        outs[5].append(wkv_s)
        outs[7].append(zr)
    y_prompt = xp.reshape(1, seq, D_MODEL)
    y_sample = xs.reshape(db, 1, D_MODEL)
    st = lambda i: jnp.stack(outs[i])
    return (y_prompt, y_sample, st(0), st(1), st(2), st(3), st(4), st(5), st(6), st(7))
```

```python
import functools
import math

import jax
import jax.numpy as jnp
from jax import lax
from jax.experimental import pallas as pl
from jax.experimental.pallas import tpu as pltpu

F32 = jnp.float32
BF16 = jnp.bfloat16
I32 = jnp.int32

D_MODEL = 1024
PAGE_SIZE = 128
RWKV_HEADS = 8
RWKV_HEAD = 64
RWKV_W = RWKV_HEADS * RWKV_HEAD
DECAY_LORA = 64
AAA_LORA = 64
GATE_LORA = 128
RWKV_PROJ = 3 * RWKV_W + DECAY_LORA + AAA_LORA + GATE_LORA
RWKV_LN_EPS = 64e-5
ATT_HEADS = 4
HEAD_QK = 64
HEAD_V = 2 * HEAD_QK
ATT_QK_W = ATT_HEADS * 2 * HEAD_QK
ATT_V_W = ATT_HEADS * HEAD_V
SUBLN_EPS = 1e-5
N_BUCKETS = 32
MAX_EXACT = N_BUCKETS // 2
MAX_DISTANCE = 128
PEER_HEADS = 8
PEER_QDIM = 256
PEER_HALF = PEER_QDIM // 2
N_KEYS = 128
N_EXPERTS = N_KEYS * N_KEYS
PEER_TOPK = 16
PLE_DIM = 256
NORM_EPS = 1e-6
NEG_INF = -1e30

LANES = 128
QUAD = 4 * RWKV_HEAD
CHUNK = 64
VMEM_LIMIT = 56 * 1024 * 1024
LOG2E = math.log2(math.e)
RWKV_PASSES = 1

NN = (((1,), (0,)), ((), ()))
NT = (((1,), (1,)), ((), ()))
TN = (((0,), (0,)), ((), ()))


def _dot(a, b, dims=NN):
    return lax.dot_general(a, b, dims, preferred_element_type=F32)


def _split(x, n):
    parts = []
    for _ in range(n):
        p = x.astype(BF16)
        parts.append(p)
        x = x - p.astype(F32)
    return parts


def _mm(a, b, dims=NN, passes=1):
    if passes == 1:
        return _dot(a.astype(BF16), b.astype(BF16), dims)
    ah, al = _split(a, 2)
    bh, bl = _split(b, 2)
    return _dot(ah, bh, dims) + (_dot(ah, bl, dims) + _dot(al, bh, dims))


def _mm_exact(a, b, dims=NN):
    a1, a2, a3 = _split(a, 3)
    return _dot(a1, b, dims) + (_dot(a2, b, dims) + _dot(a3, b, dims))


def _rms(x, g, eps):
    return x * lax.rsqrt(jnp.mean(x * x, axis=-1, keepdims=True) + eps) * g


def _full(shape):
    return pl.BlockSpec(shape, lambda *_: (0,) * len(shape))


def _params(sem):
    return pltpu.CompilerParams(dimension_semantics=sem, vmem_limit_bytes=VMEM_LIMIT)


def _t5_bucket(n):
    nf = jnp.maximum(n, 1).astype(F32)
    large = MAX_EXACT + (jnp.log(nf / MAX_EXACT) / math.log(MAX_DISTANCE / MAX_EXACT)
                         * (N_BUCKETS - MAX_EXACT)).astype(I32)
    large = jnp.minimum(large, N_BUCKETS - 1)
    return jnp.where(n < MAX_EXACT, n, large)


def _bias_kernel(tab_ref, tiles_ref, sb_ref, *, blk):
    r = lax.broadcasted_iota(I32, (blk, blk), 0)
    c = lax.broadcasted_iota(I32, (blk, blk), 1)
    for t in range(2):
        dist = c - r + t * blk
        bucket = _t5_bucket(jnp.maximum(dist, 0))
        for h in range(ATT_HEADS):
            far = tab_ref[N_BUCKETS - 1, h]
            val = jnp.zeros((blk, blk), F32)
            for b in range(N_BUCKETS - 1):
                val = jnp.where(bucket == b, (tab_ref[b, h] - far) * LOG2E, val)
            if t == 0:
                val = jnp.where(dist >= 0, val, NEG_INF)
            tiles_ref[h, t] = val
    page_w = ATT_HEADS * PAGE_SIZE
    sr = lax.broadcasted_iota(I32, (2 * ATT_HEADS, page_w + LANES), 0)
    sc = lax.broadcasted_iota(I32, (2 * ATT_HEADS, page_w + LANES), 1)
    sbucket = _t5_bucket(jnp.where(sc < page_w, PAGE_SIZE - (sc >> 2), 0))
    sval = jnp.zeros(sr.shape, F32)
    for h in range(ATT_HEADS):
        far = tab_ref[N_BUCKETS - 1, h]
        hval = jnp.zeros(sr.shape, F32)
        for b in range(N_BUCKETS - 1):
            hval = jnp.where(sbucket == b, (tab_ref[b, h] - far) * LOG2E, hval)
        sval = jnp.where((sr >> 1) == h, hval, sval)
    sb_ref[...] = sval


def _bias_tiles(rel_bias, blk):
    return pl.pallas_call(
        functools.partial(_bias_kernel, blk=blk),
        out_shape=(jax.ShapeDtypeStruct((ATT_HEADS, 2, blk, blk), F32),
                   jax.ShapeDtypeStruct((2 * ATT_HEADS, ATT_HEADS * PAGE_SIZE + LANES), F32)),
        in_specs=[pl.BlockSpec(memory_space=pltpu.SMEM)],
        name="t5_bias_tiles",
    )(rel_bias)


def _proj_kernel(x_ref, g_ref, w_ref, zr_ref, q_ref, k_ref, v_ref, kb_ref, vt_ref, gate_ref):
    h = _rms(x_ref[...], g_ref[...], NORM_EPS).astype(BF16)
    o = 0
    zr_ref[...] = _dot(h, w_ref[:, o:o + RWKV_PROJ]); o += RWKV_PROJ
    q_ref[...] = _dot(h, w_ref[:, o:o + ATT_QK_W]); o += ATT_QK_W
    k = _dot(h, w_ref[:, o:o + ATT_QK_W]); o += ATT_QK_W
    v = _dot(h, w_ref[:, o:o + ATT_V_W]); o += ATT_V_W
    tm = k.shape[0]
    for hd in range(ATT_HEADS):
        k_ref[pl.ds(hd, tm, stride=ATT_HEADS), :] = k[:, hd * LANES:(hd + 1) * LANES]
        v_ref[pl.ds(hd, tm, stride=ATT_HEADS), :] = v[:, hd * LANES:(hd + 1) * LANES]
    kb_ref[...] = k.astype(BF16)
    vt_ref[...] = v.T.astype(BF16)
    gate_ref[...] = jax.nn.sigmoid(_dot(h, w_ref[:, o:o + 2 * D_MODEL]))


def _proj(x, g, w_bf16):
    t = x.shape[0]
    tm = min(256, t)
    n_total = w_bf16.shape[1]
    row = lambda n: pl.BlockSpec((tm, n), lambda i: (i, 0))
    sds = lambda n, dt=F32: jax.ShapeDtypeStruct((t, n), dt)
    heads = pl.BlockSpec((ATT_HEADS * tm, LANES), lambda i: (i, 0))
    heads_sds = jax.ShapeDtypeStruct((ATT_HEADS * t, LANES), F32)
    return pl.pallas_call(
        _proj_kernel,
        grid=(t // tm,),
        in_specs=[row(D_MODEL), _full((1, D_MODEL)), _full((D_MODEL, n_total))],
        out_specs=(row(RWKV_PROJ), row(ATT_QK_W), heads, heads,
                   row(ATT_QK_W), pl.BlockSpec((ATT_V_W, tm), lambda i: (0, i)), row(2 * D_MODEL)),
        out_shape=(sds(RWKV_PROJ), sds(ATT_QK_W), heads_sds, heads_sds,
                   sds(ATT_QK_W, BF16), jax.ShapeDtypeStruct((ATT_V_W, t), BF16), sds(2 * D_MODEL)),
        compiler_params=_params(("parallel",)),
        name="in_proj",
    )(x, g, w_bf16)


def _rwkv_pre_kernel(*refs, per_token_prev):
    if per_token_prev:
        z_ref, zp_ref = refs[:2]
        rest = refs[2:]
    else:
        z_ref, halo_ref, s0_ref = refs[:3]
        rest = refs[3:]
    (mu_ref, w0_ref, a0_ref, kk_ref, ka_ref, w2_ref, a2_ref, g2_ref, seg_ref,
     r_out, lw_out, kx_out, v_out, kk_out, kka_out, g_out) = rest
    z = z_ref[...]
    if per_token_prev:
        zp = zp_ref[...]
    else:
        first = jnp.where(pl.program_id(0) == 0, s0_ref[...], halo_ref[7:8, :])
        rowid = lax.broadcasted_iota(I32, z.shape, 0)
        zp = jnp.where(rowid == 0, first, pltpu.roll(z, 1, axis=0))
    zs = z + (zp - z) * mu_ref[...]
    r = zs[:, 0:RWKV_W]
    k = zs[:, RWKV_W:2 * RWKV_W]
    v = zs[:, 2 * RWKV_W:3 * RWKV_W]
    wa = zs[:, 3 * RWKV_W:3 * RWKV_W + DECAY_LORA + AAA_LORA]
    g_lo = zs[:, 3 * RWKV_W + DECAY_LORA + AAA_LORA:]
    wpre = w0_ref[...] + _mm(jnp.tanh(wa), w2_ref[...], passes=3)
    nw = -wpre
    softplus = jnp.maximum(nw, 0.0) + jnp.log1p(jnp.exp(-jnp.abs(nw)))
    w = -softplus - 0.5
    a = jax.nn.sigmoid(a0_ref[...] + _mm(wa, a2_ref[...], passes=3))
    g = _mm(jax.nn.sigmoid(g_lo), g2_ref[...], passes=3)
    kk = k * kk_ref[...]
    nrm = jnp.sqrt(_mm_exact(kk * kk, seg_ref[...]))
    kk = kk / jnp.maximum(nrm, 1e-12)
    r_out[...] = r
    lw_out[...] = -jnp.exp(w)
    kx_out[...] = k * (1.0 + (a - 1.0) * ka_ref[...])
    v_out[...] = v
    kk_out[...] = kk
    kka_out[...] = kk * a
    g_out[...] = g


def _rwkv_pre(z, zprev_or_shift0, per_token_prev, wts):
    t = z.shape[0]
    tm = min(256, t)
    row = lambda n: pl.BlockSpec((tm, n), lambda i: (i, 0))
    vec = lambda n: _full((1, n))
    if per_token_prev:
        lead = [row(RWKV_PROJ), row(RWKV_PROJ)]
        args = [z, zprev_or_shift0]
    else:
        halo = pl.BlockSpec((8, RWKV_PROJ), lambda i: (jnp.maximum(i * (tm // 8) - 1, 0), 0))
        lead = [row(RWKV_PROJ), halo, vec(RWKV_PROJ)]
        args = [z, z, zprev_or_shift0]
    lora_w = DECAY_LORA + AAA_LORA
    return pl.pallas_call(
        functools.partial(_rwkv_pre_kernel, per_token_prev=per_token_prev),
        grid=(t // tm,),
        in_specs=lead + [vec(RWKV_PROJ), vec(RWKV_W), vec(RWKV_W), vec(RWKV_W), vec(RWKV_W),
                         _full((lora_w, RWKV_W)), _full((lora_w, RWKV_W)), _full((GATE_LORA, RWKV_W)),
                         _full((RWKV_W, RWKV_W))],
        out_specs=tuple(row(RWKV_W) for _ in range(7)),
        out_shape=tuple(jax.ShapeDtypeStruct((t, RWKV_W), F32) for _ in range(7)),
        compiler_params=_params(("parallel",)),
        name="rwkv_pre",
    )(*args, wts['mu'], wts['w0'], wts['a0'], wts['k_k'], wts['k_a'], wts['w2p'], wts['a2p'], wts['g2'],
      wts['seg64'])


def _stack(x, lane_head):
    return jnp.concatenate([jnp.where(lane_head == h, x, 0.0) for h in range(4)], axis=0)


def _parts(x, passes):
    return tuple(_split(x, 2)) if passes == 3 else (x.astype(BF16),)


def _mmp(pa, pb, dims=NN):
    if len(pa) == 1:
        return _dot(pa[0], pb[0], dims)
    return _dot(pa[0], pb[0], dims) + (_dot(pa[0], pb[1], dims) + _dot(pa[1], pb[0], dims))


def _rwkv_chunk_kernel(r_ref, lw_ref, kx_ref, v_ref, kk_ref, ka_ref, tri_ref, y_ref, sout_ref, s_scr,
                       *, nsub, passes):
    @pl.when(pl.program_id(0) == 0)
    def _():
        s_scr[...] = jnp.zeros_like(s_scr)

    L = CHUNK
    n4 = 4 * L
    nquad = RWKV_W // QUAD
    tri = tri_ref[...]
    lane_head = lax.broadcasted_iota(I32, (L, QUAD), 1) // RWKV_HEAD
    rr = lax.broadcasted_iota(I32, (n4, n4), 0)
    cc = lax.broadcasted_iota(I32, (n4, n4), 1)
    strict = (rr % L) > (cc % L)
    incl = (rr % L) >= (cc % L)
    eye = jnp.where(rr == cc, 1.0, 0.0)
    parts = functools.partial(_parts, passes=passes)
    insts = [(c, qd) for c in range(nsub) for qd in range(nquad)]

    st = {}
    for c, qd in insts:
        rows = slice(c * L, (c + 1) * L)
        cols = slice(qd * QUAD, (qd + 1) * QUAD)
        lw, ka, kx = lw_ref[rows, cols], ka_ref[rows, cols], kx_ref[rows, cols]
        cs = _mm_exact(tri, lw)
        cs_end = cs[L - 1:L, :]
        e_neg = jnp.exp(-cs)
        e_end = jnp.exp(cs_end - cs)
        stk = lambda x: parts(_stack(x, lane_head))
        p_a = stk(-kk_ref[rows, cols] * jnp.exp(cs - lw))
        p_b = stk(ka * e_neg)
        p_k = stk(kx * e_neg)
        p_r = stk(r_ref[rows, cols] * jnp.exp(cs))
        p_v = stk(v_ref[rows, cols])
        st[c, qd] = dict(
            p_a=p_a, p_r=p_r, p_v=p_v, p_bh=stk(ka * e_end), p_kh=stk(kx * e_end), decay=jnp.exp(cs_end),
            n_ab=jnp.where(strict, _mmp(p_a, p_b, NT), 0.0), n_ak=jnp.where(strict, _mmp(p_a, p_k, NT), 0.0),
            n_rb=jnp.where(incl, _mmp(p_r, p_b, NT), 0.0), n_rk=jnp.where(incl, _mmp(p_r, p_k, NT), 0.0))

    xs = {k: d['n_ab'] for k, d in st.items()}
    tinv = {k: eye + d['n_ab'] for k, d in st.items()}
    for _ in range(5):
        for k in insts:
            px = parts(xs[k])
            xs[k] = _mmp(px, px)
            tinv[k] = tinv[k] + _mmp(parts(tinv[k]), parts(xs[k]))

    for k in insts:
        d = st[k]
        p_t = parts(tinv[k])
        d['p_wa'] = parts(_mmp(p_t, d['p_a']))
        d['xu'] = _mmp(p_t, parts(_mmp(parts(d['n_ak']), d['p_v'])))
        d['yk'] = _mmp(parts(d['n_rk']), d['p_v'])
        d['hk'] = _mmp(d['p_v'], d['p_kh'], TN)
        d['p_rb'] = parts(d['n_rb'])

    for qd in range(nquad):
        s = s_scr[qd]
        for c in range(nsub):
            d = st[c, qd]
            p_s = parts(s)
            p_u = parts(_mmp(d['p_wa'], p_s, NT) + d['xu'])
            s_y = _mmp(d['p_r'], p_s, NT) + _mmp(d['p_rb'], p_u) + d['yk']
            y = jnp.zeros((L, QUAD), F32)
            for h in range(4):
                y = y + jnp.where(lane_head == h, s_y[h * L:(h + 1) * L, :], 0.0)
            y_ref[c * L:(c + 1) * L, qd * QUAD:(qd + 1) * QUAD] = y
            s = s * d['decay'] + _mmp(p_u, d['p_bh'], TN) + d['hk']
        s_scr[qd] = s

    @pl.when(pl.program_id(0) == pl.num_programs(0) - 1)
    def _():
        sout_ref[...] = s_scr[...]


def _rwkv_chunked(r, lw, kx, v, kk, ka, passes=RWKV_PASSES):
    t = r.shape[0]
    nsub = next(n for n in (2, 1) if t % (n * CHUNK) == 0)
    lb = nsub * CHUNK
    nq = RWKV_W // QUAD
    tri = (jnp.arange(CHUNK)[:, None] >= jnp.arange(CHUNK)[None, :]).astype(BF16)
    row = pl.BlockSpec((lb, RWKV_W), lambda i: (i, 0))
    y, s_bd = pl.pallas_call(
        functools.partial(_rwkv_chunk_kernel, nsub=nsub, passes=passes),
        grid=(t // lb,),
        in_specs=[row] * 6 + [_full((CHUNK, CHUNK))],
        out_specs=(row, _full((nq, QUAD, QUAD))),
        out_shape=(jax.ShapeDtypeStruct((t, RWKV_W), F32), jax.ShapeDtypeStruct((nq, QUAD, QUAD), F32)),
        scratch_shapes=[pltpu.VMEM((nq, QUAD, QUAD), F32)],
        compiler_params=_params(("arbitrary",)),
        name="rwkv_chunked",
    )(r, lw, kx, v, kk, ka, tri)
    blocks = [s_bd[h // 4, (h % 4) * RWKV_HEAD:(h % 4 + 1) * RWKV_HEAD, (h % 4) * RWKV_HEAD:(h % 4 + 1) * RWKV_HEAD]
              for h in range(RWKV_HEADS)]
    return y, jnp.stack(blocks)


def _rwkv_step_kernel(s_ref, r_ref, lw_ref, kx_ref, kk_ref, ka_ref, v_ref, sout_ref, y_ref):
    s = s_ref[...]
    sa = -jnp.sum(s * kk_ref[...][None], axis=1, keepdims=True)
    s_new = s * jnp.exp(lw_ref[...])[None] + sa * ka_ref[...][None] + v_ref[...][:, None, :] * kx_ref[...][None]
    sout_ref[...] = s_new
    y_ref[...] = jnp.sum(s_new * r_ref[...][None], axis=1)


def _rwkv_step(state, r, lw, kx, kk, ka, v):
    db = state.shape[0]
    vec = lambda a: a.T.reshape(RWKV_HEADS, RWKV_HEAD, db)
    st = pl.BlockSpec((None, RWKV_HEAD, RWKV_HEAD, db), lambda h: (h, 0, 0, 0))
    vs = pl.BlockSpec((None, RWKV_HEAD, db), lambda h: (h, 0, 0))
    s_new, y = pl.pallas_call(
        _rwkv_step_kernel,
        grid=(RWKV_HEADS,),
        in_specs=[st, vs, vs, vs, vs, vs, vs],
        out_specs=(st, vs),
        out_shape=(jax.ShapeDtypeStruct((RWKV_HEADS, RWKV_HEAD, RWKV_HEAD, db), F32),
                   jax.ShapeDtypeStruct((RWKV_HEADS, RWKV_HEAD, db), F32)),
        compiler_params=_params(("parallel",)),
        name="rwkv_step",
    )(state.transpose(1, 2, 3, 0), vec(r), vec(lw), vec(kx), vec(kk), vec(ka), vec(v))
    return y.reshape(RWKV_W, db).T, s_new.transpose(3, 0, 1, 2)


def _attn_prompt_kernel(lam_ref, q_ref, k_ref, vt_ref, bias_ref, o_ref, m_ref, l_ref, acc_ref, s_ref, *, blk):
    i = pl.program_id(1)
    qt = q_ref[...].T * (HEAD_QK ** -0.5 * LOG2E)
    sub = lax.broadcasted_iota(I32, qt.shape, 0)
    qt2 = jnp.concatenate([jnp.where(sub < HEAD_QK, qt, 0.0), jnp.where(sub >= HEAD_QK, qt, 0.0)],
                          axis=1).astype(BF16)
    m_ref[...] = jnp.full_like(m_ref, -jnp.inf)
    l_ref[...] = jnp.zeros_like(l_ref)
    acc_ref[...] = jnp.zeros_like(acc_ref)

    def scores(j, slot):
        start = pl.multiple_of(j * blk, blk)
        s_ref[slot] = _dot(k_ref[pl.ds(start, blk), :], qt2)

    def step(j, slot, bias, prefetch):
        if prefetch:
            scores(j + 1, 1 - slot)
        s = s_ref[slot]
        if bias is not None:
            s = s + jnp.concatenate([bias, bias], axis=1)
        m_old = m_ref[...]
        m_new = jnp.maximum(m_old, jnp.max(s, axis=0, keepdims=True))
        alpha = jnp.exp2(m_old - m_new)
        p = jnp.exp2(s - m_new)
        l_ref[...] = alpha * l_ref[...] + jnp.sum(p, axis=0, keepdims=True)
        start = pl.multiple_of(j * blk, blk)
        acc_ref[...] = alpha * acc_ref[...] + _dot(vt_ref[:, pl.ds(start, blk)], p.astype(BF16))
        m_ref[...] = m_new

    n_far = jnp.maximum(i - 1, 0)
    scores(0, 0)

    def pair_body(jj, carry):
        step(2 * jj, 0, None, True)
        step(2 * jj + 1, 1, None, True)
        return carry

    lax.fori_loop(0, n_far // 2, pair_body, 0)
    odd = (n_far % 2) == 1

    @pl.when(i == 0)
    def _():
        step(i, 0, bias_ref[0], False)

    @pl.when((i >= 1) & jnp.logical_not(odd))
    def _():
        step(i - 1, 0, bias_ref[1], True)
        step(i, 1, bias_ref[0], False)

    @pl.when((i >= 1) & odd)
    def _():
        step(i - 2, 0, None, True)
        step(i - 1, 1, bias_ref[1], True)
        step(i, 0, bias_ref[0], False)

    o = acc_ref[...] / l_ref[...]
    o_ref[...] = (o[:, :blk] - lam_ref[0] * o[:, blk:]).T


def _attn_prompt(lam, q, kb, vt, tiles, blk):
    t = q.shape[0]
    return pl.pallas_call(
        functools.partial(_attn_prompt_kernel, blk=blk),
        grid=(ATT_HEADS, t // blk),
        in_specs=[pl.BlockSpec(memory_space=pltpu.SMEM),
                  pl.BlockSpec((blk, LANES), lambda h, i: (i, h)),
                  pl.BlockSpec((t, LANES), lambda h, i: (0, h)),
                  pl.BlockSpec((HEAD_V, t), lambda h, i: (h, 0)),
                  pl.BlockSpec((None, 2, blk, blk), lambda h, i: (h, 0, 0, 0))],
        out_specs=pl.BlockSpec((blk, LANES), lambda h, i: (i, h)),
        out_shape=jax.ShapeDtypeStruct((t, ATT_V_W), F32),
        scratch_shapes=[pltpu.VMEM((1, 2 * blk), F32), pltpu.VMEM((1, 2 * blk), F32),
                        pltpu.VMEM((HEAD_V, 2 * blk), F32), pltpu.VMEM((2, blk, 2 * blk), F32)],
        compiler_params=_params(("parallel", "parallel")),
        name="attn_prompt",
    )(lam, q, kb, vt, tiles)


def _attn_sample_kernel(pt_ref, lam_ref, q_ref, kn_ref, vn_ref, sb_ref, *rest, pages_per_step):
    del pt_ref
    pp = pages_per_step
    page_w = ATT_HEADS * PAGE_SIZE
    k_refs = rest[:pp]
    v_refs = rest[pp:2 * pp]
    o_ref, m_ref, l_ref, acc_ref = rest[2 * pp:]
    g = pl.program_id(1)
    last = g == pl.num_programs(1) - 1
    row = lax.broadcasted_iota(I32, (2 * ATT_HEADS, LANES), 0)
    lane = lax.broadcasted_iota(I32, (2 * ATT_HEADS, LANES), 1)

    def rows8(x4):
        out = jnp.zeros((2 * ATT_HEADS, LANES), F32)
        for h in range(ATT_HEADS):
            out = jnp.where((row >> 1) == h, x4[h:h + 1, :], out)
        return out

    map_lanes = (row & 1) == (lane >= HEAD_QK).astype(I32)
    q8 = jnp.where(map_lanes, rows8(q_ref[0]) * (HEAD_QK ** -0.5 * LOG2E), 0.0)

    @pl.when(g == 0)
    def _():
        s_self = jnp.sum(q8 * rows8(kn_ref[0]), axis=-1, keepdims=True) + sb_ref[:, page_w:page_w + 1]
        m_ref[...] = s_self
        l_ref[...] = jnp.ones_like(l_ref)
        acc_ref[...] = rows8(vn_ref[0])

    qb = q8.astype(BF16)
    ss = [_dot(qb, k_refs[p][...].astype(BF16), NT) for p in range(pp)]
    ss[-1] = ss[-1] + jnp.where(last, sb_ref[:, :page_w], 0.0)
    s_all = jnp.concatenate(ss, axis=-1)
    col_head = lax.broadcasted_iota(I32, s_all.shape, 1) & (ATT_HEADS - 1)
    row_head = lax.broadcasted_iota(I32, s_all.shape, 0) >> 1
    s_all = jnp.where(col_head == row_head, s_all, -jnp.inf)
    m_old = m_ref[...]
    m_new = jnp.maximum(m_old, jnp.max(s_all, axis=-1, keepdims=True))
    alpha = jnp.exp2(m_old - m_new)
    p_all = jnp.exp2(s_all - m_new).astype(BF16)
    l_ref[...] = alpha * l_ref[...] + jnp.sum(p_all.astype(F32), axis=-1, keepdims=True)
    acc = alpha * acc_ref[...]
    for p in range(pp):
        acc = acc + _dot(p_all[:, p * page_w:(p + 1) * page_w], v_refs[p][...].astype(BF16))
    acc_ref[...] = acc
    m_ref[...] = m_new

    @pl.when(last)
    def _():
        o = acc_ref[...] / l_ref[...]
        o_ref[0] = jnp.concatenate([o[2 * h:2 * h + 1, :] - lam_ref[0] * o[2 * h + 1:2 * h + 2, :]
                                    for h in range(ATT_HEADS)], axis=0)


def _attn_sample(lam, q, k_new, v_new, cache_k, cache_v, layer, page_table, sbias):
    db, n_pages = page_table.shape
    pp = next(p for p in (16, 8, 1) if n_pages % p == 0)
    page_w = ATT_HEADS * PAGE_SIZE
    tok = pl.BlockSpec((1, ATT_HEADS, LANES), lambda b, g, pt: (b, 0, 0))

    def page_spec(p):
        return pl.BlockSpec((None, None, page_w, LANES), lambda b, g, pt: (layer, pt[b, g * pp + p], 0, 0))

    pages = lambda c: c.reshape(c.shape[0], c.shape[1], page_w, LANES)
    shp = (db, ATT_HEADS, LANES)
    out = pl.pallas_call(
        functools.partial(_attn_sample_kernel, pages_per_step=pp),
        grid_spec=pltpu.PrefetchScalarGridSpec(
            num_scalar_prefetch=1,
            grid=(db, n_pages // pp),
            in_specs=[pl.BlockSpec(memory_space=pltpu.SMEM), tok, tok, tok, _full(sbias.shape)]
                     + [page_spec(p) for p in range(pp)] * 2,
            out_specs=tok,
            scratch_shapes=[pltpu.VMEM((2 * ATT_HEADS, 1), F32), pltpu.VMEM((2 * ATT_HEADS, 1), F32),
                            pltpu.VMEM((2 * ATT_HEADS, LANES), F32)]),
        out_shape=jax.ShapeDtypeStruct(shp, F32),
        compiler_params=_params(("parallel", "arbitrary")),
        name="attn_sample",
    )(page_table, lam, q.reshape(shp), k_new.reshape(shp), v_new.reshape(shp), sbias,
      *([pages(cache_k)] * pp), *([pages(cache_v)] * pp))
    return out.reshape(db, ATT_V_W)


def _merge_kernel(x_ref, y_ref, r_ref, kx_ref, v_ref, g_ref, o_ref, gate_ref,
                  lnw_ref, lnb_ref, rk_ref, sub_ref, seg_ref, wbr_ref, wba_ref, wout_ref, nffn_ref,
                  x1_ref, xnt_ref, *, attn_scale):
    seg = seg_ref[...]
    y = y_ref[...]
    inv_n = 1.0 / RWKV_HEAD
    mean = _mm_exact(y, seg) * inv_n
    yc = y - mean
    var = _mm_exact(yc * yc, seg) * inv_n
    yn = yc * lax.rsqrt(var + RWKV_LN_EPS) * lnw_ref[...] + lnb_ref[...]
    v = v_ref[...]
    bonus = _mm_exact(r_ref[...] * kx_ref[...] * rk_ref[...], seg) * v
    y_rwkv = (yn + bonus) * g_ref[...]
    o = o_ref[...]
    parts = []
    for h in range(ATT_HEADS):
        oh = o[:, h * HEAD_V:(h + 1) * HEAD_V]
        parts.append(_rms(oh, sub_ref[...], SUBLN_EPS) * attn_scale)
    y_attn = jnp.concatenate(parts, axis=-1)
    gates = gate_ref[...]
    merged = (gates[:, :D_MODEL] * _mm(y_rwkv, wbr_ref[...])
              + gates[:, D_MODEL:] * _mm(y_attn, wba_ref[...]))
    x1 = x_ref[...] + _mm(merged, wout_ref[...])
    x1_ref[...] = x1
    xnt_ref[...] = _rms(x1, nffn_ref[...], NORM_EPS).T.astype(BF16)


def _merge(x, y, r, kx, v, g, o, gates, wts, attn_scale):
    t = x.shape[0]
    tm = min(256, t)
    row = lambda n: pl.BlockSpec((tm, n), lambda i: (i, 0))
    vec = lambda n: _full((1, n))
    return pl.pallas_call(
        functools.partial(_merge_kernel, attn_scale=attn_scale),
        grid=(t // tm,),
        in_specs=[row(D_MODEL)] + [row(RWKV_W)] * 6 + [row(2 * D_MODEL),
                  vec(RWKV_W), vec(RWKV_W), vec(RWKV_W), vec(HEAD_V), _full((RWKV_W, RWKV_W)),
                  _full((RWKV_W, D_MODEL)), _full((ATT_V_W, D_MODEL)), _full((D_MODEL, D_MODEL)),
                  vec(D_MODEL)],
        out_specs=(row(D_MODEL), pl.BlockSpec((D_MODEL, tm), lambda i: (0, i))),
        out_shape=(jax.ShapeDtypeStruct((t, D_MODEL), F32), jax.ShapeDtypeStruct((D_MODEL, t), BF16)),
        compiler_params=_params(("parallel",)),
        name="merge",
    )(x, y, r, kx, v, g, o, gates, wts['ln_w'], wts['ln_b'], wts['r_k'], wts['subln'], wts['seg64'],
      wts['w_br_rwkv'], wts['w_br_attn'], wts['w_out'], wts['norm_ffn'])


def _oddeven_merge(lo, hi, r):
    step = r * 2
    if step < hi - lo:
        yield from _oddeven_merge(lo, hi, step)
        yield from _oddeven_merge(lo + r, hi, step)
        yield from [(i, i + r) for i in range(lo + r, hi - r, step)]
    else:
        yield (lo, lo + r)


def _oddeven_sort(lo, hi):
    if hi - lo >= 1:
        mid = lo + (hi - lo) // 2
        yield from _oddeven_sort(lo, mid)
        yield from _oddeven_sort(mid + 1, hi)
        yield from _oddeven_merge(lo, hi, 1)


SUBLANES = 8
_SORT16 = tuple(_oddeven_sort(0, 15))


def _top_sorted(vals, k):
    ngroups = vals.shape[0] // SUBLANES
    lv = [vals[g * SUBLANES:(g + 1) * SUBLANES] for g in range(ngroups)] + [None] * (16 - ngroups)
    for i, j in _SORT16:
        a, b = lv[i], lv[j]
        if b is None:
            continue
        if a is None:
            lv[i], lv[j] = b, None
        else:
            lv[i], lv[j] = jnp.maximum(a, b), jnp.minimum(a, b)
    lv = lv[:k]
    ninf = jnp.full(lv[0].shape, -jnp.inf, F32)
    rows = []
    for it in range(k):
        mx = jnp.max(lv[0], axis=0, keepdims=True)
        rows.append(mx)
        hit = lv[0] == mx
        for l in range(k - it - 1):
            if lv[l] is None:
                break
            nxt = lv[l + 1] if (l + 1 < len(lv) and lv[l + 1] is not None) else ninf
            lv[l] = jnp.where(hit, nxt, lv[l])
    return rows


def _peer_route_kernel(xnt_ref, wq_ref, k1_ref, k2_ref, rank_ref, nsel_ref, e1_ref, e2_ref):
    half = PEER_TOPK // 2
    qt = _dot(wq_ref[...], xnt_ref[...])
    k1 = k1_ref[...].astype(BF16)
    k2 = k2_ref[...].astype(BF16)
    for h in range(PEER_HEADS):
        base = h * PEER_QDIM
        s1 = _dot(k1, qt[base:base + PEER_HALF, :].astype(BF16))
        s2 = _dot(k2, qt[base + PEER_HALF:base + PEER_QDIM, :].astype(BF16))
        v1 = _top_sorted(s1, PEER_TOPK)
        v2 = _top_sorted(s2, PEER_TOPK)
        v1m = jnp.concatenate(v1, axis=0)
        v2m = jnp.concatenate(v2, axis=0)
        groups = ([v1[0] + v2m] + [v1[i] + v2m[:half] for i in range(1, half)] + [v1m[half:] + v2[0]])
        cand = jnp.concatenate(groups, axis=0)
        tau = _top_sorted(cand, PEER_TOPK)[-1]
        top = v1[0] + v2[0]
        z = jnp.sum(jnp.where(cand >= tau, jnp.exp(cand - top), 0.0), axis=0, keepdims=True)
        count = lambda grp: jnp.sum(jnp.where(grp >= tau, 1.0, 0.0), axis=0, keepdims=True)
        n_rank = [count(groups[i]) for i in range(half)]
        tail = jnp.where(groups[half] >= tau, 1.0, 0.0)
        n_rank += [tail[i:i + 1, :] for i in range(half)]
        nsel = jnp.zeros(s1.shape, F32)
        rank2 = jnp.full(s2.shape, float(PEER_TOPK), F32)
        for i in range(PEER_TOPK):
            nsel = jnp.where(s1 == v1[i], n_rank[i], nsel)
            rank2 = jnp.where(s2 == v2[i], float(i), rank2)
        rank_ref[h] = rank2.astype(BF16)
        nsel_ref[h] = nsel
        e1_ref[h] = jnp.exp(s1 - v1[0]) / z
        e2_ref[h] = jnp.exp(s2 - v2[0]).astype(BF16)


def _peer_tokens(t):
    return min(512, t)


def _peer_route(xnt, wts):
    t = xnt.shape[1]
    tb = _peer_tokens(t)
    col = pl.BlockSpec((PEER_HEADS, N_KEYS, tb), lambda i: (0, 0, i))
    big = lambda dt: jax.ShapeDtypeStruct((PEER_HEADS, N_KEYS, t), dt)
    return pl.pallas_call(
        _peer_route_kernel,
        grid=(t // tb,),
        in_specs=[pl.BlockSpec((D_MODEL, tb), lambda i: (0, i)),
                  _full((PEER_HEADS * PEER_QDIM, D_MODEL)),
                  _full((N_KEYS, PEER_HALF)), _full((N_KEYS, PEER_HALF))],
        out_specs=(col, col, col, col),
        out_shape=(big(BF16), big(F32), big(F32), big(BF16)),
        compiler_params=_params(("parallel",)),
        name="peer_route",
    )(xnt, wts['wqT'], wts['peer_k1'], wts['peer_k2'])


ROW_TILE = 16
PEER_EB = 1024


def _peer_dense_kernel(xnt_ref, rank_ref, nsel_ref, e1_ref, e2_ref, eu_ref, evt_ref, x1_ref,
                       pe_ref, wple_ref, nple_ref, wgate_ref, nfin_ref,
                       o_ref, yt_ref, u_ref, act_ref, gated_ref, bn_ref, be1_ref, *, eb, final_norm):
    j = pl.program_id(1)

    @pl.when(j == 0)
    def _():
        yt_ref[...] = jnp.zeros_like(yt_ref)
        act_ref[...] = jnp.zeros_like(act_ref)

    xnt = xnt_ref[...]
    u_ref[...] = _dot(eu_ref[...], xnt)
    groups = eb // N_KEYS
    tb = xnt.shape[1]
    zero = jnp.zeros((ROW_TILE, tb), BF16)
    live = jnp.where(j >= 1, 1.0, 0.0)
    prev = jnp.maximum(j - 1, 0)
    for a in range(groups):
        i1 = prev * groups + a
        for h in range(PEER_HEADS):
            bn_ref[a, h] = jnp.broadcast_to(nsel_ref[h, pl.ds(i1, 1), :], (ROW_TILE, tb)).astype(BF16)
            be1_ref[a, h] = jnp.broadcast_to(e1_ref[h, pl.ds(i1, 1), :] * live, (ROW_TILE, tb)).astype(BF16)
        for rt in range(N_KEYS // ROW_TILE):
            rows = slice(rt * ROW_TILE, (rt + 1) * ROW_TILE)
            w = None
            for h in range(PEER_HEADS):
                wh = jnp.where(rank_ref[h, rows, :] < bn_ref[a, h], e2_ref[h, rows, :], zero) * be1_ref[a, h]
                w = wh if w is None else w + wh
            erows = slice(a * N_KEYS + rt * ROW_TILE, a * N_KEYS + (rt + 1) * ROW_TILE)
            gated_ref[erows, :] = w * act_ref[erows, :]
    yt_ref[...] += _dot(evt_ref[...], gated_ref[...])
    u = u_ref[...]
    act_ref[...] = (0.5 * u * (1.0 + lax.erf(u * (2.0 ** -0.5)))).astype(BF16)

    @pl.when(j == pl.num_programs(1) - 1)
    def _():
        x = x1_ref[...] + yt_ref[...].T
        gate = jax.nn.sigmoid(_mm(_rms(x, nple_ref[...], NORM_EPS), wgate_ref[...]))
        x = x + _mm(pe_ref[...], wple_ref[...]) * gate
        o_ref[...] = _rms(x, nfin_ref[...], NORM_EPS) if final_norm else x


def _peer_dense(xnt, route, x1, pe, wts, final_norm):
    t = x1.shape[0]
    tb = _peer_tokens(t)
    eb = PEER_EB
    nblk = N_EXPERTS // eb
    col = pl.BlockSpec((PEER_HEADS, N_KEYS, tb), lambda i, j: (0, 0, i))
    return pl.pallas_call(
        functools.partial(_peer_dense_kernel, eb=eb, final_norm=final_norm),
        grid=(t // tb, nblk + 1),
        in_specs=[pl.BlockSpec((D_MODEL, tb), lambda i, j: (0, i)), col, col, col, col,
                  pl.BlockSpec((eb, D_MODEL), lambda i, j: (jnp.minimum(j, nblk - 1), 0)),
                  pl.BlockSpec((None, D_MODEL, eb), lambda i, j: (jnp.maximum(j - 1, 0), 0, 0)),
                  pl.BlockSpec((tb, D_MODEL), lambda i, j: (i, 0)),
                  pl.BlockSpec((tb, PLE_DIM), lambda i, j: (i, 0)),
                  _full((PLE_DIM, D_MODEL)), _full((1, D_MODEL)), _full((D_MODEL, D_MODEL)),
                  _full((1, D_MODEL))],
        out_specs=pl.BlockSpec((tb, D_MODEL), lambda i, j: (i, 0)),
        out_shape=jax.ShapeDtypeStruct((t, D_MODEL), F32),
        scratch_shapes=[pltpu.VMEM((D_MODEL, tb), F32), pltpu.VMEM((eb, tb), F32),
                        pltpu.VMEM((eb, tb), BF16), pltpu.VMEM((eb, tb), BF16),
                        pltpu.VMEM((eb // N_KEYS, PEER_HEADS, ROW_TILE, tb), BF16),
                        pltpu.VMEM((eb // N_KEYS, PEER_HEADS, ROW_TILE, tb), BF16)],
        compiler_params=_params(("parallel", "arbitrary")),
        name="peer_dense",
    )(xnt, *route, wts['eu'], wts['evT'], x1, pe,
      wts['w_ple'], wts['norm_ple'], wts['w_ple_gate'], wts['norm_final'])


def _token_stages(x, pe, y_rwkv_raw, pre, o_attn, gates, wts, attn_scale, final_norm):
    r, _, kx, v, _, _, g = pre
    x1, xnt = _merge(x, y_rwkv_raw, r, kx, v, g, o_attn, gates, wts, attn_scale)
    return _peer_dense(xnt, _peer_route(xnt, wts), x1, pe, wts, final_norm)


def kernel(x_prompt, x_sample, cache_k, cache_v, state_wkv, state_shift, page_table, p_prompt, p_sample,
           norm_mix, w_in, rwkv_mu, rwkv_w0, rwkv_w2, rwkv_a0, rwkv_a2, rwkv_g2, rwkv_k_k, rwkv_k_a,
           rwkv_r_k, rwkv_ln_w, rwkv_ln_b, attn_lq1, attn_lk1, attn_lq2, attn_lk2, attn_subln, rel_bias,
           w_br_rwkv, w_br_attn, w_out, norm_ffn, peer_wq, peer_k1, peer_k2, peer_u, peer_v,
           norm_ple, w_ple, w_ple_gate, norm_final):
    depth = w_in.shape[0]
    bsz, seq = x_prompt.shape[0], x_prompt.shape[1]
    db, dseq = x_sample.shape[0], x_sample.shape[1]
    assert bsz == 1 and dseq == 1, "prompt batch and decode length are fixed at 1"
    blk = next((b for b in (512, 256) if seq % b == 0), seq)
    xp = x_prompt.reshape(seq, D_MODEL)
    xs = x_sample.reshape(db, D_MODEL)
    tiles, sbias = _bias_tiles(rel_bias, blk)
    seg64 = (jnp.arange(RWKV_W)[:, None] // RWKV_HEAD == jnp.arange(RWKV_W)[None, :] // RWKV_HEAD).astype(BF16)
    row = lambda a: a.reshape(1, -1)
    outs = [[] for _ in range(8)]
    for l in range(depth):
        zpad = jnp.zeros((DECAY_LORA, RWKV_W), F32)
        wts = {
            'mu': row(rwkv_mu[l]), 'w0': row(rwkv_w0[l]), 'a0': row(rwkv_a0[l]),
            'k_k': row(rwkv_k_k[l]), 'k_a': row(rwkv_k_a[l]),
            'w2p': jnp.concatenate([rwkv_w2[l], zpad], axis=0),
            'a2p': jnp.concatenate([zpad, rwkv_a2[l]], axis=0),
            'g2': rwkv_g2[l], 'seg64': seg64,
            'ln_w': row(rwkv_ln_w[l]), 'ln_b': row(rwkv_ln_b[l]), 'r_k': row(rwkv_r_k[l]),
            'subln': row(attn_subln[l]),
            'w_br_rwkv': w_br_rwkv[l].astype(BF16), 'w_br_attn': w_br_attn[l].astype(BF16),
            'w_out': w_out[l].astype(BF16), 'norm_ffn': row(norm_ffn[l]),
            'wqT': peer_wq[l].T.astype(BF16),
            'peer_k1': peer_k1[l], 'peer_k2': peer_k2[l],
            'eu': peer_u[l].astype(BF16),
            'evT': peer_v[l].reshape(N_EXPERTS // PEER_EB, PEER_EB, D_MODEL).transpose(0, 2, 1).astype(BF16),
            'w_ple': w_ple[l].astype(BF16), 'norm_ple': row(norm_ple[l]),
            'w_ple_gate': w_ple_gate[l].astype(BF16), 'norm_final': row(norm_final),
        }
        lam_init = 0.8 - 0.6 * math.exp(-0.3 * l)
        lam = (jnp.exp(jnp.sum(attn_lq1[l] * attn_lk1[l])) - jnp.exp(jnp.sum(attn_lq2[l] * attn_lk2[l]))
               + lam_init).reshape(1).astype(F32)
        w_in_b = w_in[l].astype(BF16)
        g_mix = row(norm_mix[l])

        zr, q, k, v, kb, vt, gates = _proj(xp, g_mix, w_in_b)
        pre = _rwkv_pre(zr, jnp.zeros((1, RWKV_PROJ), F32), False, wts)
        y_raw, wkv_p = _rwkv_chunked(*pre[:6])
        o = _attn_prompt(lam, q, kb, vt, tiles, blk)
        xp = _token_stages(xp, p_prompt[l].reshape(seq, PLE_DIM), y_raw, pre, o, gates, wts,
                           1.0 - lam_init, l == depth - 1)
        outs[0].append(k.reshape(1, seq, ATT_HEADS, 2 * HEAD_QK))
        outs[1].append(v.reshape(1, seq, ATT_HEADS, HEAD_V))
        outs[4].append(wkv_p[None])
        outs[6].append(zr[seq - 1:seq, :])

        zr, q, k, v, _, _, gates = _proj(xs, g_mix, w_in_b)
        pre = _rwkv_pre(zr, state_shift[l], True, wts)
        r, lw, kx, vv, kk, ka, _ = pre
        y_raw, wkv_s = _rwkv_step(state_wkv[l], r, lw, kx, kk, ka, vv)
        o = _attn_sample(lam, q, k, v, cache_k, cache_v, l, page_table, sbias)
        xs = _token_stages(xs, p_sample[l].reshape(db, PLE_DIM), y_raw, pre, o, gates, wts,
                           1.0 - lam_init, l == depth - 1)
        outs[2].append(k.reshape(db, 1, ATT_HEADS, 2 * HEAD_QK))
        outs[3].append(v.reshape(db, 1, ATT_HEADS, HEAD_V))
        outs[5].append(wkv_s)
        outs[7].append(zr)
    y_prompt = xp.reshape(1, seq, D_MODEL)
    y_sample = xs.reshape(db, 1, D_MODEL)
    st = lambda i: jnp.stack(outs[i])
    return (y_prompt, y_sample, st(0), st(1), st(2), st(3), st(4), st(5), st(6), st(7))
```

```python
import functools
import math

import jax
import jax.numpy as jnp
from jax import lax
from jax.experimental import pallas as pl
from jax.experimental.pallas import tpu as pltpu

F32 = jnp.float32
BF16 = jnp.bfloat16
I32 = jnp.int32

D_MODEL = 1024
PAGE_SIZE = 128
RWKV_HEADS = 8
RWKV_HEAD = 64
RWKV_W = RWKV_HEADS * RWKV_HEAD
DECAY_LORA = 64
AAA_LORA = 64
GATE_LORA = 128
RWKV_PROJ = 3 * RWKV_W + DECAY_LORA + AAA_LORA + GATE_LORA
RWKV_LN_EPS = 64e-5
ATT_HEADS = 4
HEAD_QK = 64
HEAD_V = 2 * HEAD_QK
ATT_QK_W = ATT_HEADS * 2 * HEAD_QK
ATT_V_W = ATT_HEADS * HEAD_V
SUBLN_EPS = 1e-5
N_BUCKETS = 32
MAX_EXACT = N_BUCKETS // 2
MAX_DISTANCE = 128
PEER_HEADS = 8
PEER_QDIM = 256
PEER_HALF = PEER_QDIM // 2
N_KEYS = 128
N_EXPERTS = N_KEYS * N_KEYS
PEER_TOPK = 16
PLE_DIM = 256
NORM_EPS = 1e-6
NEG_INF = -1e30

LANES = 128
QUAD = 4 * RWKV_HEAD
CHUNK = 64
VMEM_LIMIT = 56 * 1024 * 1024
LOG2E = math.log2(math.e)
RWKV_PASSES = 1

NN = (((1,), (0,)), ((), ()))
NT = (((1,), (1,)), ((), ()))
TN = (((0,), (0,)), ((), ()))


def _dot(a, b, dims=NN):
    return lax.dot_general(a, b, dims, preferred_element_type=F32)


def _split(x, n):
    parts = []
    for _ in range(n):
        p = x.astype(BF16)
        parts.append(p)
        x = x - p.astype(F32)
    return parts


def _mm(a, b, dims=NN, passes=1):
    if passes == 1:
        return _dot(a.astype(BF16), b.astype(BF16), dims)
    ah, al = _split(a, 2)
    bh, bl = _split(b, 2)
    return _dot(ah, bh, dims) + (_dot(ah, bl, dims) + _dot(al, bh, dims))


def _mm_exact(a, b, dims=NN):
    a1, a2, a3 = _split(a, 3)
    return _dot(a1, b, dims) + (_dot(a2, b, dims) + _dot(a3, b, dims))


def _rms(x, g, eps):
    return x * lax.rsqrt(jnp.mean(x * x, axis=-1, keepdims=True) + eps) * g


def _full(shape):
    return pl.BlockSpec(shape, lambda *_: (0,) * len(shape))


def _params(sem):
    return pltpu.CompilerParams(dimension_semantics=sem, vmem_limit_bytes=VMEM_LIMIT)


def _t5_bucket(n):
    nf = jnp.maximum(n, 1).astype(F32)
    large = MAX_EXACT + (jnp.log(nf / MAX_EXACT) / math.log(MAX_DISTANCE / MAX_EXACT)
                         * (N_BUCKETS - MAX_EXACT)).astype(I32)
    large = jnp.minimum(large, N_BUCKETS - 1)
    return jnp.where(n < MAX_EXACT, n, large)


def _bias_kernel(tab_ref, tiles_ref, sb_ref, *, blk):
    r = lax.broadcasted_iota(I32, (blk, blk), 0)
    c = lax.broadcasted_iota(I32, (blk, blk), 1)
    for t in range(2):
        dist = c - r + t * blk
        bucket = _t5_bucket(jnp.maximum(dist, 0))
        for h in range(ATT_HEADS):
            far = tab_ref[N_BUCKETS - 1, h]
            val = jnp.zeros((blk, blk), F32)
            for b in range(N_BUCKETS - 1):
                val = jnp.where(bucket == b, (tab_ref[b, h] - far) * LOG2E, val)
            if t == 0:
                val = jnp.where(dist >= 0, val, NEG_INF)
            tiles_ref[h, t] = val
    page_w = ATT_HEADS * PAGE_SIZE
    sr = lax.broadcasted_iota(I32, (2 * ATT_HEADS, page_w + LANES), 0)
    sc = lax.broadcasted_iota(I32, (2 * ATT_HEADS, page_w + LANES), 1)
    sbucket = _t5_bucket(jnp.where(sc < page_w, PAGE_SIZE - (sc >> 2), 0))
    sval = jnp.zeros(sr.shape, F32)
    for h in range(ATT_HEADS):
        far = tab_ref[N_BUCKETS - 1, h]
        hval = jnp.zeros(sr.shape, F32)
        for b in range(N_BUCKETS - 1):
            hval = jnp.where(sbucket == b, (tab_ref[b, h] - far) * LOG2E, hval)
        sval = jnp.where((sr >> 1) == h, hval, sval)
    sb_ref[...] = sval


def _bias_tiles(rel_bias, blk):
    return pl.pallas_call(
        functools.partial(_bias_kernel, blk=blk),
        out_shape=(jax.ShapeDtypeStruct((ATT_HEADS, 2, blk, blk), F32),
                   jax.ShapeDtypeStruct((2 * ATT_HEADS, ATT_HEADS * PAGE_SIZE + LANES), F32)),
        in_specs=[pl.BlockSpec(memory_space=pltpu.SMEM)],
        name="t5_bias_tiles",
    )(rel_bias)


def _proj_kernel(x_ref, g_ref, w_ref, zr_ref, q_ref, k_ref, v_ref, kb_ref, vt_ref, gate_ref):
    h = _rms(x_ref[...], g_ref[...], NORM_EPS).astype(BF16)
    o = 0
    zr_ref[...] = _dot(h, w_ref[:, o:o + RWKV_PROJ]); o += RWKV_PROJ
    q_ref[...] = _dot(h, w_ref[:, o:o + ATT_QK_W]); o += ATT_QK_W
    k = _dot(h, w_ref[:, o:o + ATT_QK_W]); o += ATT_QK_W
    v = _dot(h, w_ref[:, o:o + ATT_V_W]); o += ATT_V_W
    tm = k.shape[0]
    for hd in range(ATT_HEADS):
        k_ref[pl.ds(hd, tm, stride=ATT_HEADS), :] = k[:, hd * LANES:(hd + 1) * LANES]
        v_ref[pl.ds(hd, tm, stride=ATT_HEADS), :] = v[:, hd * LANES:(hd + 1) * LANES]
    kb_ref[...] = k.astype(BF16)
    vt_ref[...] = v.T.astype(BF16)
    gate_ref[...] = jax.nn.sigmoid(_dot(h, w_ref[:, o:o + 2 * D_MODEL]))


def _proj(x, g, w_bf16):
    t = x.shape[0]
    tm = min(256, t)
    n_total = w_bf16.shape[1]
    row = lambda n: pl.BlockSpec((tm, n), lambda i: (i, 0))
    sds = lambda n, dt=F32: jax.ShapeDtypeStruct((t, n), dt)
    heads = pl.BlockSpec((ATT_HEADS * tm, LANES), lambda i: (i, 0))
    heads_sds = jax.ShapeDtypeStruct((ATT_HEADS * t, LANES), F32)
    return pl.pallas_call(
        _proj_kernel,
        grid=(t // tm,),
        in_specs=[row(D_MODEL), _full((1, D_MODEL)), _full((D_MODEL, n_total))],
        out_specs=(row(RWKV_PROJ), row(ATT_QK_W), heads, heads,
                   row(ATT_QK_W), pl.BlockSpec((ATT_V_W, tm), lambda i: (0, i)), row(2 * D_MODEL)),
        out_shape=(sds(RWKV_PROJ), sds(ATT_QK_W), heads_sds, heads_sds,
                   sds(ATT_QK_W, BF16), jax.ShapeDtypeStruct((ATT_V_W, t), BF16), sds(2 * D_MODEL)),
        compiler_params=_params(("parallel",)),
        name="in_proj",
    )(x, g, w_bf16)


def _rwkv_pre_kernel(*refs, per_token_prev):
    if per_token_prev:
        z_ref, zp_ref = refs[:2]
        rest = refs[2:]
    else:
        z_ref, halo_ref, s0_ref = refs[:3]
        rest = refs[3:]
    (mu_ref, w0_ref, a0_ref, kk_ref, ka_ref, w2_ref, a2_ref, g2_ref, seg_ref,
     r_out, lw_out, kx_out, v_out, kk_out, kka_out, g_out) = rest
    z = z_ref[...]
    if per_token_prev:
        zp = zp_ref[...]
    else:
        first = jnp.where(pl.program_id(0) == 0, s0_ref[...], halo_ref[7:8, :])
        rowid = lax.broadcasted_iota(I32, z.shape, 0)
        zp = jnp.where(rowid == 0, first, pltpu.roll(z, 1, axis=0))
    zs = z + (zp - z) * mu_ref[...]
    r = zs[:, 0:RWKV_W]
    k = zs[:, RWKV_W:2 * RWKV_W]
    v = zs[:, 2 * RWKV_W:3 * RWKV_W]
    wa = zs[:, 3 * RWKV_W:3 * RWKV_W + DECAY_LORA + AAA_LORA]
    g_lo = zs[:, 3 * RWKV_W + DECAY_LORA + AAA_LORA:]
    wpre = w0_ref[...] + _mm(jnp.tanh(wa), w2_ref[...], passes=3)
    nw = -wpre
    softplus = jnp.maximum(nw, 0.0) + jnp.log1p(jnp.exp(-jnp.abs(nw)))
    w = -softplus - 0.5
    a = jax.nn.sigmoid(a0_ref[...] + _mm(wa, a2_ref[...], passes=3))
    g = _mm(jax.nn.sigmoid(g_lo), g2_ref[...], passes=3)
    kk = k * kk_ref[...]
    nrm = jnp.sqrt(_mm_exact(kk * kk, seg_ref[...]))
    kk = kk / jnp.maximum(nrm, 1e-12)
    r_out[...] = r
    lw_out[...] = -jnp.exp(w)
    kx_out[...] = k * (1.0 + (a - 1.0) * ka_ref[...])
    v_out[...] = v
    kk_out[...] = kk
    kka_out[...] = kk * a
    g_out[...] = g


def _rwkv_pre(z, zprev_or_shift0, per_token_prev, wts):
    t = z.shape[0]
    tm = min(256, t)
    row = lambda n: pl.BlockSpec((tm, n), lambda i: (i, 0))
    vec = lambda n: _full((1, n))
    if per_token_prev:
        lead = [row(RWKV_PROJ), row(RWKV_PROJ)]
        args = [z, zprev_or_shift0]
    else:
        halo = pl.BlockSpec((8, RWKV_PROJ), lambda i: (jnp.maximum(i * (tm // 8) - 1, 0), 0))
        lead = [row(RWKV_PROJ), halo, vec(RWKV_PROJ)]
        args = [z, z, zprev_or_shift0]
    lora_w = DECAY_LORA + AAA_LORA
    return pl.pallas_call(
        functools.partial(_rwkv_pre_kernel, per_token_prev=per_token_prev),
        grid=(t // tm,),
        in_specs=lead + [vec(RWKV_PROJ), vec(RWKV_W), vec(RWKV_W), vec(RWKV_W), vec(RWKV_W),
                         _full((lora_w, RWKV_W)), _full((lora_w, RWKV_W)), _full((GATE_LORA, RWKV_W)),
                         _full((RWKV_W, RWKV_W))],
        out_specs=tuple(row(RWKV_W) for _ in range(7)),
        out_shape=tuple(jax.ShapeDtypeStruct((t, RWKV_W), F32) for _ in range(7)),
        compiler_params=_params(("parallel",)),
        name="rwkv_pre",
    )(*args, wts['mu'], wts['w0'], wts['a0'], wts['k_k'], wts['k_a'], wts['w2p'], wts['a2p'], wts['g2'],
      wts['seg64'])


def _stack(x, lane_head):
    return jnp.concatenate([jnp.where(lane_head == h, x, 0.0) for h in range(4)], axis=0)


def _parts(x, passes):
    return tuple(_split(x, 2)) if passes == 3 else (x.astype(BF16),)


def _mmp(pa, pb, dims=NN):
    if len(pa) == 1:
        return _dot(pa[0], pb[0], dims)
    return _dot(pa[0], pb[0], dims) + (_dot(pa[0], pb[1], dims) + _dot(pa[1], pb[0], dims))


def _rwkv_chunk_kernel(r_ref, lw_ref, kx_ref, v_ref, kk_ref, ka_ref, tri_ref, y_ref, sout_ref, s_scr,
                       *, nsub, passes):
    @pl.when(pl.program_id(0) == 0)
    def _():
        s_scr[...] = jnp.zeros_like(s_scr)

    L = CHUNK
    n4 = 4 * L
    nquad = RWKV_W // QUAD
    tri = tri_ref[...]
    lane_head = lax.broadcasted_iota(I32, (L, QUAD), 1) // RWKV_HEAD
    rr = lax.broadcasted_iota(I32, (n4, n4), 0)
    cc = lax.broadcasted_iota(I32, (n4, n4), 1)
    strict = (rr % L) > (cc % L)
    incl = (rr % L) >= (cc % L)
    eye = jnp.where(rr == cc, 1.0, 0.0)
    parts = functools.partial(_parts, passes=passes)
    insts = [(c, qd) for c in range(nsub) for qd in range(nquad)]

    st = {}
    for c, qd in insts:
        rows = slice(c * L, (c + 1) * L)
        cols = slice(qd * QUAD, (qd + 1) * QUAD)
        lw, ka, kx = lw_ref[rows, cols], ka_ref[rows, cols], kx_ref[rows, cols]
        cs = _mm_exact(tri, lw)
        cs_end = cs[L - 1:L, :]
        e_neg = jnp.exp(-cs)
        e_end = jnp.exp(cs_end - cs)
        stk = lambda x: parts(_stack(x, lane_head))
        p_a = stk(-kk_ref[rows, cols] * jnp.exp(cs - lw))
        p_b = stk(ka * e_neg)
        p_k = stk(kx * e_neg)
        p_r = stk(r_ref[rows, cols] * jnp.exp(cs))
        p_v = stk(v_ref[rows, cols])
        st[c, qd] = dict(
            p_a=p_a, p_r=p_r, p_v=p_v, p_bh=stk(ka * e_end), p_kh=stk(kx * e_end), decay=jnp.exp(cs_end),
            n_ab=jnp.where(strict, _mmp(p_a, p_b, NT), 0.0), n_ak=jnp.where(strict, _mmp(p_a, p_k, NT), 0.0),
            n_rb=jnp.where(incl, _mmp(p_r, p_b, NT), 0.0), n_rk=jnp.where(incl, _mmp(p_r, p_k, NT), 0.0))

    xs = {k: d['n_ab'] for k, d in st.items()}
    tinv = {k: eye + d['n_ab'] for k, d in st.items()}
    for _ in range(5):
        for k in insts:
            px = parts(xs[k])
            xs[k] = _mmp(px, px)
            tinv[k] = tinv[k] + _mmp(parts(tinv[k]), parts(xs[k]))

    for k in insts:
        d = st[k]
        p_t = parts(tinv[k])
        d['p_wa'] = parts(_mmp(p_t, d['p_a']))
        d['xu'] = _mmp(p_t, parts(_mmp(parts(d['n_ak']), d['p_v'])))
        d['yk'] = _mmp(parts(d['n_rk']), d['p_v'])
        d['hk'] = _mmp(d['p_v'], d['p_kh'], TN)
        d['p_rb'] = parts(d['n_rb'])

    for qd in range(nquad):
        s = s_scr[qd]
        for c in range(nsub):
            d = st[c, qd]
            p_s = parts(s)
            p_u = parts(_mmp(d['p_wa'], p_s, NT) + d['xu'])
            s_y = _mmp(d['p_r'], p_s, NT) + _mmp(d['p_rb'], p_u) + d['yk']
            y = jnp.zeros((L, QUAD), F32)
            for h in range(4):
                y = y + jnp.where(lane_head == h, s_y[h * L:(h + 1) * L, :], 0.0)
            y_ref[c * L:(c + 1) * L, qd * QUAD:(qd + 1) * QUAD] = y
            s = s * d['decay'] + _mmp(p_u, d['p_bh'], TN) + d['hk']
        s_scr[qd] = s

    @pl.when(pl.program_id(0) == pl.num_programs(0) - 1)
    def _():
        sout_ref[...] = s_scr[...]


def _rwkv_chunked(r, lw, kx, v, kk, ka, passes=RWKV_PASSES):
    t = r.shape[0]
    nsub = next(n for n in (2, 1) if t % (n * CHUNK) == 0)
    lb = nsub * CHUNK
    nq = RWKV_W // QUAD
    tri = (jnp.arange(CHUNK)[:, None] >= jnp.arange(CHUNK)[None, :]).astype(BF16)
    row = pl.BlockSpec((lb, RWKV_W), lambda i: (i, 0))
    y, s_bd = pl.pallas_call(
        functools.partial(_rwkv_chunk_kernel, nsub=nsub, passes=passes),
        grid=(t // lb,),
        in_specs=[row] * 6 + [_full((CHUNK, CHUNK))],
        out_specs=(row, _full((nq, QUAD, QUAD))),
        out_shape=(jax.ShapeDtypeStruct((t, RWKV_W), F32), jax.ShapeDtypeStruct((nq, QUAD, QUAD), F32)),
        scratch_shapes=[pltpu.VMEM((nq, QUAD, QUAD), F32)],
        compiler_params=_params(("arbitrary",)),
        name="rwkv_chunked",
    )(r, lw, kx, v, kk, ka, tri)
    blocks = [s_bd[h // 4, (h % 4) * RWKV_HEAD:(h % 4 + 1) * RWKV_HEAD, (h % 4) * RWKV_HEAD:(h % 4 + 1) * RWKV_HEAD]
              for h in range(RWKV_HEADS)]
    return y, jnp.stack(blocks)


def _rwkv_step_kernel(s_ref, r_ref, lw_ref, kx_ref, kk_ref, ka_ref, v_ref, sout_ref, y_ref):
    s = s_ref[...]
    sa = -jnp.sum(s * kk_ref[...][None], axis=1, keepdims=True)
    s_new = s * jnp.exp(lw_ref[...])[None] + sa * ka_ref[...][None] + v_ref[...][:, None, :] * kx_ref[...][None]
    sout_ref[...] = s_new
    y_ref[...] = jnp.sum(s_new * r_ref[...][None], axis=1)


def _rwkv_step(state, r, lw, kx, kk, ka, v):
    db = state.shape[0]
    vec = lambda a: a.T.reshape(RWKV_HEADS, RWKV_HEAD, db)
    st = pl.BlockSpec((None, RWKV_HEAD, RWKV_HEAD, db), lambda h: (h, 0, 0, 0))
    vs = pl.BlockSpec((None, RWKV_HEAD, db), lambda h: (h, 0, 0))
    s_new, y = pl.pallas_call(
        _rwkv_step_kernel,
        grid=(RWKV_HEADS,),
        in_specs=[st, vs, vs, vs, vs, vs, vs],
        out_specs=(st, vs),
        out_shape=(jax.ShapeDtypeStruct((RWKV_HEADS, RWKV_HEAD, RWKV_HEAD, db), F32),
                   jax.ShapeDtypeStruct((RWKV_HEADS, RWKV_HEAD, db), F32)),
        compiler_params=_params(("parallel",)),
        name="rwkv_step",
    )(state.transpose(1, 2, 3, 0), vec(r), vec(lw), vec(kx), vec(kk), vec(ka), vec(v))
    return y.reshape(RWKV_W, db).T, s_new.transpose(3, 0, 1, 2)


def _attn_prompt_kernel(lam_ref, q_ref, k_ref, vt_ref, bias_ref, o_ref, m_ref, l_ref, acc_ref, s_ref, *, blk):
    i = pl.program_id(1)
    qt = q_ref[...].T * (HEAD_QK ** -0.5 * LOG2E)
    sub = lax.broadcasted_iota(I32, qt.shape, 0)
    qt2 = jnp.concatenate([jnp.where(sub < HEAD_QK, qt, 0.0), jnp.where(sub >= HEAD_QK, qt, 0.0)],
                          axis=1).astype(BF16)
    m_ref[...] = jnp.full_like(m_ref, -jnp.inf)
    l_ref[...] = jnp.zeros_like(l_ref)
    acc_ref[...] = jnp.zeros_like(acc_ref)

    def scores(j, slot):
        start = pl.multiple_of(j * blk, blk)
        s_ref[slot] = _dot(k_ref[pl.ds(start, blk), :], qt2)

    def step(j, slot, bias, prefetch):
        if prefetch:
            scores(j + 1, 1 - slot)
        s = s_ref[slot]
        if bias is not None:
            s = s + jnp.concatenate([bias, bias], axis=1)
        m_old = m_ref[...]
        m_new = jnp.maximum(m_old, jnp.max(s, axis=0, keepdims=True))
        alpha = jnp.exp2(m_old - m_new)
        p = jnp.exp2(s - m_new)
        l_ref[...] = alpha * l_ref[...] + jnp.sum(p, axis=0, keepdims=True)
        start = pl.multiple_of(j * blk, blk)
        acc_ref[...] = alpha * acc_ref[...] + _dot(vt_ref[:, pl.ds(start, blk)], p.astype(BF16))
        m_ref[...] = m_new

    n_far = jnp.maximum(i - 1, 0)
    scores(0, 0)

    def pair_body(jj, carry):
        step(2 * jj, 0, None, True)
        step(2 * jj + 1, 1, None, True)
        return carry

    lax.fori_loop(0, n_far // 2, pair_body, 0)
    odd = (n_far % 2) == 1

    @pl.when(i == 0)
    def _():
        step(i, 0, bias_ref[0], False)

    @pl.when((i >= 1) & jnp.logical_not(odd))
    def _():
        step(i - 1, 0, bias_ref[1], True)
        step(i, 1, bias_ref[0], False)

    @pl.when((i >= 1) & odd)
    def _():
        step(i - 2, 0, None, True)
        step(i - 1, 1, bias_ref[1], True)
        step(i, 0, bias_ref[0], False)

    o = acc_ref[...] / l_ref[...]
    o_ref[...] = (o[:, :blk] - lam_ref[0] * o[:, blk:]).T


def _attn_prompt(lam, q, kb, vt, tiles, blk):
    t = q.shape[0]
    return pl.pallas_call(
        functools.partial(_attn_prompt_kernel, blk=blk),
        grid=(ATT_HEADS, t // blk),
        in_specs=[pl.BlockSpec(memory_space=pltpu.SMEM),
                  pl.BlockSpec((blk, LANES), lambda h, i: (i, h)),
                  pl.BlockSpec((t, LANES), lambda h, i: (0, h)),
                  pl.BlockSpec((HEAD_V, t), lambda h, i: (h, 0)),
                  pl.BlockSpec((None, 2, blk, blk), lambda h, i: (h, 0, 0, 0))],
        out_specs=pl.BlockSpec((blk, LANES), lambda h, i: (i, h)),
        out_shape=jax.ShapeDtypeStruct((t, ATT_V_W), F32),
        scratch_shapes=[pltpu.VMEM((1, 2 * blk), F32), pltpu.VMEM((1, 2 * blk), F32),
                        pltpu.VMEM((HEAD_V, 2 * blk), F32), pltpu.VMEM((2, blk, 2 * blk), F32)],
        compiler_params=_params(("parallel", "parallel")),
        name="attn_prompt",
    )(lam, q, kb, vt, tiles)


def _attn_sample_kernel(pt_ref, lam_ref, q_ref, kn_ref, vn_ref, sb_ref, *rest, pages_per_step):
    del pt_ref
    pp = pages_per_step
    page_w = ATT_HEADS * PAGE_SIZE
    k_refs = rest[:pp]
    v_refs = rest[pp:2 * pp]
    o_ref, m_ref, l_ref, acc_ref = rest[2 * pp:]
    g = pl.program_id(1)
    last = g == pl.num_programs(1) - 1
    row = lax.broadcasted_iota(I32, (2 * ATT_HEADS, LANES), 0)
    lane = lax.broadcasted_iota(I32, (2 * ATT_HEADS, LANES), 1)

    def rows8(x4):
        out = jnp.zeros((2 * ATT_HEADS, LANES), F32)
        for h in range(ATT_HEADS):
            out = jnp.where((row >> 1) == h, x4[h:h + 1, :], out)
        return out

    map_lanes = (row & 1) == (lane >= HEAD_QK).astype(I32)
    q8 = jnp.where(map_lanes, rows8(q_ref[0]) * (HEAD_QK ** -0.5 * LOG2E), 0.0)

    @pl.when(g == 0)
    def _():
        s_self = jnp.sum(q8 * rows8(kn_ref[0]), axis=-1, keepdims=True) + sb_ref[:, page_w:page_w + 1]
        m_ref[...] = s_self
        l_ref[...] = jnp.ones_like(l_ref)
        acc_ref[...] = rows8(vn_ref[0])

    qb = q8.astype(BF16)
    ss = [_dot(qb, k_refs[p][...].astype(BF16), NT) for p in range(pp)]
    ss[-1] = ss[-1] + jnp.where(last, sb_ref[:, :page_w], 0.0)
    s_all = jnp.concatenate(ss, axis=-1)
    col_head = lax.broadcasted_iota(I32, s_all.shape, 1) & (ATT_HEADS - 1)
    row_head = lax.broadcasted_iota(I32, s_all.shape, 0) >> 1
    s_all = jnp.where(col_head == row_head, s_all, -jnp.inf)
    m_old = m_ref[...]
    m_new = jnp.maximum(m_old, jnp.max(s_all, axis=-1, keepdims=True))
    alpha = jnp.exp2(m_old - m_new)
    p_all = jnp.exp2(s_all - m_new).astype(BF16)
    l_ref[...] = alpha * l_ref[...] + jnp.sum(p_all.astype(F32), axis=-1, keepdims=True)
    acc = alpha * acc_ref[...]
    for p in range(pp):
        acc = acc + _dot(p_all[:, p * page_w:(p + 1) * page_w], v_refs[p][...].astype(BF16))
    acc_ref[...] = acc
    m_ref[...] = m_new

    @pl.when(last)
    def _():
        o = acc_ref[...] / l_ref[...]
        o_ref[0] = jnp.concatenate([o[2 * h:2 * h + 1, :] - lam_ref[0] * o[2 * h + 1:2 * h + 2, :]
                                    for h in range(ATT_HEADS)], axis=0)


def _attn_sample(lam, q, k_new, v_new, cache_k, cache_v, layer, page_table, sbias):
    db, n_pages = page_table.shape
    pp = next(p for p in (16, 8, 1) if n_pages % p == 0)
    page_w = ATT_HEADS * PAGE_SIZE
    tok = pl.BlockSpec((1, ATT_HEADS, LANES), lambda b, g, pt: (b, 0, 0))

    def page_spec(p):
        return pl.BlockSpec((None, None, page_w, LANES), lambda b, g, pt: (layer, pt[b, g * pp + p], 0, 0))

    pages = lambda c: c.reshape(c.shape[0], c.shape[1], page_w, LANES)
    shp = (db, ATT_HEADS, LANES)
    out = pl.pallas_call(
        functools.partial(_attn_sample_kernel, pages_per_step=pp),
        grid_spec=pltpu.PrefetchScalarGridSpec(
            num_scalar_prefetch=1,
            grid=(db, n_pages // pp),
            in_specs=[pl.BlockSpec(memory_space=pltpu.SMEM), tok, tok, tok, _full(sbias.shape)]
                     + [page_spec(p) for p in range(pp)] * 2,
            out_specs=tok,
            scratch_shapes=[pltpu.VMEM((2 * ATT_HEADS, 1), F32), pltpu.VMEM((2 * ATT_HEADS, 1), F32),
                            pltpu.VMEM((2 * ATT_HEADS, LANES), F32)]),
        out_shape=jax.ShapeDtypeStruct(shp, F32),
        compiler_params=_params(("parallel", "arbitrary")),
        name="attn_sample",
    )(page_table, lam, q.reshape(shp), k_new.reshape(shp), v_new.reshape(shp), sbias,
      *([pages(cache_k)] * pp), *([pages(cache_v)] * pp))
    return out.reshape(db, ATT_V_W)


def _merge_kernel(x_ref, y_ref, r_ref, kx_ref, v_ref, g_ref, o_ref, gate_ref,
                  lnw_ref, lnb_ref, rk_ref, sub_ref, seg_ref, wbr_ref, wba_ref, wout_ref, nffn_ref,
                  x1_ref, xnt_ref, *, attn_scale):
    seg = seg_ref[...]
    y = y_ref[...]
    inv_n = 1.0 / RWKV_HEAD
    mean = _mm_exact(y, seg) * inv_n
    yc = y - mean
    var = _mm_exact(yc * yc, seg) * inv_n
    yn = yc * lax.rsqrt(var + RWKV_LN_EPS) * lnw_ref[...] + lnb_ref[...]
    v = v_ref[...]
    bonus = _mm_exact(r_ref[...] * kx_ref[...] * rk_ref[...], seg) * v
    y_rwkv = (yn + bonus) * g_ref[...]
    o = o_ref[...]
    parts = []
    for h in range(ATT_HEADS):
        oh = o[:, h * HEAD_V:(h + 1) * HEAD_V]
        parts.append(_rms(oh, sub_ref[...], SUBLN_EPS) * attn_scale)
    y_attn = jnp.concatenate(parts, axis=-1)
    gates = gate_ref[...]
    merged = (gates[:, :D_MODEL] * _mm(y_rwkv, wbr_ref[...])
              + gates[:, D_MODEL:] * _mm(y_attn, wba_ref[...]))
    x1 = x_ref[...] + _mm(merged, wout_ref[...])
    x1_ref[...] = x1
    xnt_ref[...] = _rms(x1, nffn_ref[...], NORM_EPS).T.astype(BF16)


def _merge(x, y, r, kx, v, g, o, gates, wts, attn_scale):
    t = x.shape[0]
    tm = min(256, t)
    row = lambda n: pl.BlockSpec((tm, n), lambda i: (i, 0))
    vec = lambda n: _full((1, n))
    return pl.pallas_call(
        functools.partial(_merge_kernel, attn_scale=attn_scale),
        grid=(t // tm,),
        in_specs=[row(D_MODEL)] + [row(RWKV_W)] * 6 + [row(2 * D_MODEL),
                  vec(RWKV_W), vec(RWKV_W), vec(RWKV_W), vec(HEAD_V), _full((RWKV_W, RWKV_W)),
                  _full((RWKV_W, D_MODEL)), _full((ATT_V_W, D_MODEL)), _full((D_MODEL, D_MODEL)),
                  vec(D_MODEL)],
        out_specs=(row(D_MODEL), pl.BlockSpec((D_MODEL, tm), lambda i: (0, i))),
        out_shape=(jax.ShapeDtypeStruct((t, D_MODEL), F32), jax.ShapeDtypeStruct((D_MODEL, t), BF16)),
        compiler_params=_params(("parallel",)),
        name="merge",
    )(x, y, r, kx, v, g, o, gates, wts['ln_w'], wts['ln_b'], wts['r_k'], wts['subln'], wts['seg64'],
      wts['w_br_rwkv'], wts['w_br_attn'], wts['w_out'], wts['norm_ffn'])


def _oddeven_merge(lo, hi, r):
    step = r * 2
    if step < hi - lo:
        yield from _oddeven_merge(lo, hi, step)
        yield from _oddeven_merge(lo + r, hi, step)
        yield from [(i, i + r) for i in range(lo + r, hi - r, step)]
    else:
        yield (lo, lo + r)


def _oddeven_sort(lo, hi):
    if hi - lo >= 1:
        mid = lo + (hi - lo) // 2
        yield from _oddeven_sort(lo, mid)
        yield from _oddeven_sort(mid + 1, hi)
        yield from _oddeven_merge(lo, hi, 1)


SUBLANES = 8
_SORT16 = tuple(_oddeven_sort(0, 15))


def _top_sorted(vals, k):
    ngroups = vals.shape[0] // SUBLANES
    lv = [vals[g * SUBLANES:(g + 1) * SUBLANES] for g in range(ngroups)] + [None] * (16 - ngroups)
    for i, j in _SORT16:
        a, b = lv[i], lv[j]
        if b is None:
            continue
        if a is None:
            lv[i], lv[j] = b, None
        else:
            lv[i], lv[j] = jnp.maximum(a, b), jnp.minimum(a, b)
    lv = lv[:k]
    ninf = jnp.full(lv[0].shape, -jnp.inf, F32)
    rows = []
    for it in range(k):
        mx = jnp.max(lv[0], axis=0, keepdims=True)
        rows.append(mx)
        hit = lv[0] == mx
        for l in range(k - it - 1):
            if lv[l] is None:
                break
            nxt = lv[l + 1] if (l + 1 < len(lv) and lv[l + 1] is not None) else ninf
            lv[l] = jnp.where(hit, nxt, lv[l])
    return rows


def _peer_route_kernel(xnt_ref, wq_ref, k1_ref, k2_ref, rank_ref, nsel_ref, e1_ref, e2_ref):
    half = PEER_TOPK // 2
    qt = _dot(wq_ref[...], xnt_ref[...])
    k1 = k1_ref[...].astype(BF16)
    k2 = k2_ref[...].astype(BF16)
    for h in range(PEER_HEADS):
        base = h * PEER_QDIM
        s1 = _dot(k1, qt[base:base + PEER_HALF, :].astype(BF16))
        s2 = _dot(k2, qt[base + PEER_HALF:base + PEER_QDIM, :].astype(BF16))
        v1 = _top_sorted(s1, PEER_TOPK)
        v2 = _top_sorted(s2, PEER_TOPK)
        v1m = jnp.concatenate(v1, axis=0)
        v2m = jnp.concatenate(v2, axis=0)
        groups = ([v1[0] + v2m] + [v1[i] + v2m[:half] for i in range(1, half)] + [v1m[half:] + v2[0]])
        cand = jnp.concatenate(groups, axis=0)
        tau = _top_sorted(cand, PEER_TOPK)[-1]
        top = v1[0] + v2[0]
        z = jnp.sum(jnp.where(cand >= tau, jnp.exp(cand - top), 0.0), axis=0, keepdims=True)
        count = lambda grp: jnp.sum(jnp.where(grp >= tau, 1.0, 0.0), axis=0, keepdims=True)
        n_rank = [count(groups[i]) for i in range(half)]
        tail = jnp.where(groups[half] >= tau, 1.0, 0.0)
        n_rank += [tail[i:i + 1, :] for i in range(half)]
        nsel = jnp.zeros(s1.shape, F32)
        rank2 = jnp.full(s2.shape, float(PEER_TOPK), F32)
        for i in range(PEER_TOPK):
            nsel = jnp.where(s1 == v1[i], n_rank[i], nsel)
            rank2 = jnp.where(s2 == v2[i], float(i), rank2)
        rank_ref[h] = rank2.astype(BF16)
        nsel_ref[h] = nsel
        e1_ref[h] = jnp.exp(s1 - v1[0]) / z
        e2_ref[h] = jnp.exp(s2 - v2[0]).astype(BF16)


def _peer_tokens(t):
    return min(512, t)


def _peer_route(xnt, wts):
    t = xnt.shape[1]
    tb = _peer_tokens(t)
    col = pl.BlockSpec((PEER_HEADS, N_KEYS, tb), lambda i: (0, 0, i))
    big = lambda dt: jax.ShapeDtypeStruct((PEER_HEADS, N_KEYS, t), dt)
    return pl.pallas_call(
        _peer_route_kernel,
        grid=(t // tb,),
        in_specs=[pl.BlockSpec((D_MODEL, tb), lambda i: (0, i)),
                  _full((PEER_HEADS * PEER_QDIM, D_MODEL)),
                  _full((N_KEYS, PEER_HALF)), _full((N_KEYS, PEER_HALF))],
        out_specs=(col, col, col, col),
        out_shape=(big(BF16), big(F32), big(F32), big(BF16)),
        compiler_params=_params(("parallel",)),
        name="peer_route",
    )(xnt, wts['wqT'], wts['peer_k1'], wts['peer_k2'])


ROW_TILE = 16
GROUP_BLOCK = 4
WEIGHT_STREAMS = 2
PEER_EB = 1024


def _peer_dense_kernel(xnt_ref, rank_ref, nsel_ref, e1_ref, e2_ref, x1_ref,
                       pe_ref, wple_ref, nple_ref, wgate_ref, nfin_ref, *rest, eb, final_norm):
    eu_refs = rest[:WEIGHT_STREAMS]
    evt_refs = rest[WEIGHT_STREAMS:2 * WEIGHT_STREAMS]
    o_ref, yt_ref, u_ref, act_ref, gated_ref, bn_ref, be1_ref = rest[2 * WEIGHT_STREAMS:]
    j = pl.program_id(1)

    @pl.when(j == 0)
    def _():
        yt_ref[...] = jnp.zeros_like(yt_ref)
        act_ref[...] = jnp.zeros_like(act_ref)

    xnt = xnt_ref[...]
    ue = eb // WEIGHT_STREAMS
    for c in range(WEIGHT_STREAMS):
        u_ref[c * ue:(c + 1) * ue, :] = _dot(eu_refs[c][...], xnt)
    groups = eb // N_KEYS
    tb = xnt.shape[1]
    zero = jnp.zeros((ROW_TILE, tb), BF16)
    live = jnp.where(j >= 1, 1.0, 0.0)
    prev = jnp.maximum(j - 1, 0)
    for a in range(groups):
        i1 = prev * groups + a
        for h in range(PEER_HEADS):
            bn_ref[a, h] = jnp.broadcast_to(nsel_ref[h, pl.ds(i1, 1), :], (ROW_TILE, tb)).astype(BF16)
            be1_ref[a, h] = jnp.broadcast_to(e1_ref[h, pl.ds(i1, 1), :] * live, (ROW_TILE, tb)).astype(BF16)
    for a0 in range(0, groups, GROUP_BLOCK):
        for rt in range(N_KEYS // ROW_TILE):
            rows = slice(rt * ROW_TILE, (rt + 1) * ROW_TILE)
            w = [None] * GROUP_BLOCK
            for h in range(PEER_HEADS):
                rank = rank_ref[h, rows, :]
                e2 = e2_ref[h, rows, :]
                for g in range(GROUP_BLOCK):
                    wh = jnp.where(rank < bn_ref[a0 + g, h], e2, zero) * be1_ref[a0 + g, h]
                    w[g] = wh if w[g] is None else w[g] + wh
            for g in range(GROUP_BLOCK):
                erows = slice((a0 + g) * N_KEYS + rt * ROW_TILE, (a0 + g) * N_KEYS + (rt + 1) * ROW_TILE)
                gated_ref[erows, :] = w[g] * act_ref[erows, :]
    gated = gated_ref[...]
    dq = D_MODEL // WEIGHT_STREAMS
    for c in range(WEIGHT_STREAMS):
        yt_ref[c * dq:(c + 1) * dq, :] += _dot(evt_refs[c][...], gated)
    u = u_ref[...]
    act_ref[...] = (0.5 * u * (1.0 + lax.erf(u * (2.0 ** -0.5)))).astype(BF16)

    @pl.when(j == pl.num_programs(1) - 1)
    def _():
        x = x1_ref[...] + yt_ref[...].T
        gate = jax.nn.sigmoid(_mm(_rms(x, nple_ref[...], NORM_EPS), wgate_ref[...]))
        x = x + _mm(pe_ref[...], wple_ref[...]) * gate
        o_ref[...] = _rms(x, nfin_ref[...], NORM_EPS) if final_norm else x


def _peer_dense(xnt, route, x1, pe, wts, final_norm):
    t = x1.shape[0]
    tb = _peer_tokens(t)
    eb = PEER_EB
    ws = WEIGHT_STREAMS
    nblk = N_EXPERTS // eb
    col = pl.BlockSpec((PEER_HEADS, N_KEYS, tb), lambda i, j: (0, 0, i))
    return pl.pallas_call(
        functools.partial(_peer_dense_kernel, eb=eb, final_norm=final_norm),
        grid=(t // tb, nblk + 1),
        in_specs=[pl.BlockSpec((D_MODEL, tb), lambda i, j: (0, i)), col, col, col, col,
                  pl.BlockSpec((tb, D_MODEL), lambda i, j: (i, 0)),
                  pl.BlockSpec((tb, PLE_DIM), lambda i, j: (i, 0)),
                  _full((PLE_DIM, D_MODEL)), _full((1, D_MODEL)), _full((D_MODEL, D_MODEL)),
                  _full((1, D_MODEL))]
                 + [pl.BlockSpec((eb // ws, D_MODEL), lambda i, j, c=c: (jnp.minimum(j, nblk - 1) * ws + c, 0))
                    for c in range(ws)]
                 + [pl.BlockSpec((None, D_MODEL // ws, eb), lambda i, j, c=c: (jnp.maximum(j - 1, 0), c, 0))
                    for c in range(ws)],
        out_specs=pl.BlockSpec((tb, D_MODEL), lambda i, j: (i, 0)),
        out_shape=jax.ShapeDtypeStruct((t, D_MODEL), F32),
        scratch_shapes=[pltpu.VMEM((D_MODEL, tb), F32), pltpu.VMEM((eb, tb), F32),
                        pltpu.VMEM((eb, tb), BF16), pltpu.VMEM((eb, tb), BF16),
                        pltpu.VMEM((eb // N_KEYS, PEER_HEADS, ROW_TILE, tb), BF16),
                        pltpu.VMEM((eb // N_KEYS, PEER_HEADS, ROW_TILE, tb), BF16)],
        compiler_params=_params(("parallel", "arbitrary")),
        name="peer_dense",
    )(xnt, *route, x1, pe, wts['w_ple'], wts['norm_ple'], wts['w_ple_gate'], wts['norm_final'],
      *([wts['eu']] * ws), *([wts['evT']] * ws))


def _token_stages(x, pe, y_rwkv_raw, pre, o_attn, gates, wts, attn_scale, final_norm):
    r, _, kx, v, _, _, g = pre
    x1, xnt = _merge(x, y_rwkv_raw, r, kx, v, g, o_attn, gates, wts, attn_scale)
    return _peer_dense(xnt, _peer_route(xnt, wts), x1, pe, wts, final_norm)


def kernel(x_prompt, x_sample, cache_k, cache_v, state_wkv, state_shift, page_table, p_prompt, p_sample,
           norm_mix, w_in, rwkv_mu, rwkv_w0, rwkv_w2, rwkv_a0, rwkv_a2, rwkv_g2, rwkv_k_k, rwkv_k_a,
           rwkv_r_k, rwkv_ln_w, rwkv_ln_b, attn_lq1, attn_lk1, attn_lq2, attn_lk2, attn_subln, rel_bias,
           w_br_rwkv, w_br_attn, w_out, norm_ffn, peer_wq, peer_k1, peer_k2, peer_u, peer_v,
           norm_ple, w_ple, w_ple_gate, norm_final):
    depth = w_in.shape[0]
    bsz, seq = x_prompt.shape[0], x_prompt.shape[1]
    db, dseq = x_sample.shape[0], x_sample.shape[1]
    assert bsz == 1 and dseq == 1, "prompt batch and decode length are fixed at 1"
    blk = next((b for b in (512, 256) if seq % b == 0), seq)
    xp = x_prompt.reshape(seq, D_MODEL)
    xs = x_sample.reshape(db, D_MODEL)
    tiles, sbias = _bias_tiles(rel_bias, blk)
    seg64 = (jnp.arange(RWKV_W)[:, None] // RWKV_HEAD == jnp.arange(RWKV_W)[None, :] // RWKV_HEAD).astype(BF16)
    row = lambda a: a.reshape(1, -1)
    outs = [[] for _ in range(8)]
    for l in range(depth):
        zpad = jnp.zeros((DECAY_LORA, RWKV_W), F32)
        wts = {
            'mu': row(rwkv_mu[l]), 'w0': row(rwkv_w0[l]), 'a0': row(rwkv_a0[l]),
            'k_k': row(rwkv_k_k[l]), 'k_a': row(rwkv_k_a[l]),
            'w2p': jnp.concatenate([rwkv_w2[l], zpad], axis=0),
            'a2p': jnp.concatenate([zpad, rwkv_a2[l]], axis=0),
            'g2': rwkv_g2[l], 'seg64': seg64,
            'ln_w': row(rwkv_ln_w[l]), 'ln_b': row(rwkv_ln_b[l]), 'r_k': row(rwkv_r_k[l]),
            'subln': row(attn_subln[l]),
            'w_br_rwkv': w_br_rwkv[l].astype(BF16), 'w_br_attn': w_br_attn[l].astype(BF16),
            'w_out': w_out[l].astype(BF16), 'norm_ffn': row(norm_ffn[l]),
            'wqT': peer_wq[l].T.astype(BF16),
            'peer_k1': peer_k1[l], 'peer_k2': peer_k2[l],
            'eu': peer_u[l].astype(BF16),
            'evT': peer_v[l].reshape(N_EXPERTS // PEER_EB, PEER_EB, D_MODEL).transpose(0, 2, 1).astype(BF16),
            'w_ple': w_ple[l].astype(BF16), 'norm_ple': row(norm_ple[l]),
            'w_ple_gate': w_ple_gate[l].astype(BF16), 'norm_final': row(norm_final),
        }
        lam_init = 0.8 - 0.6 * math.exp(-0.3 * l)
        lam = (jnp.exp(jnp.sum(attn_lq1[l] * attn_lk1[l])) - jnp.exp(jnp.sum(attn_lq2[l] * attn_lk2[l]))
               + lam_init).reshape(1).astype(F32)
        w_in_b = w_in[l].astype(BF16)
        g_mix = row(norm_mix[l])

        zr, q, k, v, kb, vt, gates = _proj(xp, g_mix, w_in_b)
        pre = _rwkv_pre(zr, jnp.zeros((1, RWKV_PROJ), F32), False, wts)
        y_raw, wkv_p = _rwkv_chunked(*pre[:6])
        o = _attn_prompt(lam, q, kb, vt, tiles, blk)
        xp = _token_stages(xp, p_prompt[l].reshape(seq, PLE_DIM), y_raw, pre, o, gates, wts,
                           1.0 - lam_init, l == depth - 1)
        outs[0].append(k.reshape(1, seq, ATT_HEADS, 2 * HEAD_QK))
        outs[1].append(v.reshape(1, seq, ATT_HEADS, HEAD_V))
        outs[4].append(wkv_p[None])
        outs[6].append(zr[seq - 1:seq, :])

        zr, q, k, v, _, _, gates = _proj(xs, g_mix, w_in_b)
        pre = _rwkv_pre(zr, state_shift[l], True, wts)
        r, lw, kx, vv, kk, ka, _ = pre
        y_raw, wkv_s = _rwkv_step(state_wkv[l], r, lw, kx, kk, ka, vv)
        o = _attn_sample(lam, q, k, v, cache_k, cache_v, l, page_table, sbias)
        xs = _token_stages(xs, p_sample[l].reshape(db, PLE_DIM), y_raw, pre, o, gates, wts,
                           1.0 - lam_init, l == depth - 1)
        outs[2].append(k.reshape(db, 1, ATT_HEADS, 2 * HEAD_QK))
        outs[3].append(v.reshape(db, 1, ATT_HEADS, HEAD_V))
        outs[5].append(wkv_s)
        outs[7].append(zr)
    y_prompt = xp.reshape(1, seq, D_MODEL)
    y_sample = xs.reshape(db, 1, D_MODEL)
    st = lambda i: jnp.stack(outs[i])
    return (y_prompt, y_sample, st(0), st(1), st(2), st(3), st(4), st(5), st(6), st(7))
```

```python
import functools
import math

import jax
import jax.numpy as jnp
from jax import lax
from jax.experimental import pallas as pl
from jax.experimental.pallas import tpu as pltpu

F32 = jnp.float32
BF16 = jnp.bfloat16
I32 = jnp.int32

D_MODEL = 1024
PAGE_SIZE = 128
RWKV_HEADS = 8
RWKV_HEAD = 64
RWKV_W = RWKV_HEADS * RWKV_HEAD
DECAY_LORA = 64
AAA_LORA = 64
GATE_LORA = 128
RWKV_PROJ = 3 * RWKV_W + DECAY_LORA + AAA_LORA + GATE_LORA
RWKV_LN_EPS = 64e-5
ATT_HEADS = 4
HEAD_QK = 64
HEAD_V = 2 * HEAD_QK
ATT_QK_W = ATT_HEADS * 2 * HEAD_QK
ATT_V_W = ATT_HEADS * HEAD_V
SUBLN_EPS = 1e-5
N_BUCKETS = 32
MAX_EXACT = N_BUCKETS // 2
MAX_DISTANCE = 128
PEER_HEADS = 8
PEER_QDIM = 256
PEER_HALF = PEER_QDIM // 2
N_KEYS = 128
N_EXPERTS = N_KEYS * N_KEYS
PEER_TOPK = 16
PLE_DIM = 256
NORM_EPS = 1e-6
NEG_INF = -1e30

LANES = 128
MXU_TILE = 256
QUAD = 4 * RWKV_HEAD
CHUNK = 64
VMEM_LIMIT = 56 * 1024 * 1024
LOG2E = math.log2(math.e)
RWKV_PASSES = 1

NN = (((1,), (0,)), ((), ()))
NT = (((1,), (1,)), ((), ()))
TN = (((0,), (0,)), ((), ()))


def _dot(a, b, dims=NN):
    return lax.dot_general(a, b, dims, preferred_element_type=F32)


def _split(x, n):
    parts = []
    for _ in range(n):
        p = x.astype(BF16)
        parts.append(p)
        x = x - p.astype(F32)
    return parts


def _mm(a, b, dims=NN, passes=1):
    if passes == 1:
        return _dot(a.astype(BF16), b.astype(BF16), dims)
    ah, al = _split(a, 2)
    bh, bl = _split(b, 2)
    return _dot(ah, bh, dims) + (_dot(ah, bl, dims) + _dot(al, bh, dims))


def _mm_exact(a, b, dims=NN):
    a1, a2, a3 = _split(a, 3)
    return _dot(a1, b, dims) + (_dot(a2, b, dims) + _dot(a3, b, dims))


def _seg_sum(x, seg):
    n = x.shape[1]
    return jnp.concatenate([_mm_exact(x[:, c:c + MXU_TILE], seg[c:c + MXU_TILE, c:c + MXU_TILE])
                            for c in range(0, n, MXU_TILE)], axis=1)


def _rms(x, g, eps):
    return x * lax.rsqrt(jnp.mean(x * x, axis=-1, keepdims=True) + eps) * g


def _full(shape):
    return pl.BlockSpec(shape, lambda *_: (0,) * len(shape))


def _params(sem):
    return pltpu.CompilerParams(dimension_semantics=sem, vmem_limit_bytes=VMEM_LIMIT)


def _t5_bucket(n):
    nf = jnp.maximum(n, 1).astype(F32)
    large = MAX_EXACT + (jnp.log(nf / MAX_EXACT) / math.log(MAX_DISTANCE / MAX_EXACT)
                         * (N_BUCKETS - MAX_EXACT)).astype(I32)
    large = jnp.minimum(large, N_BUCKETS - 1)
    return jnp.where(n < MAX_EXACT, n, large)


def _bias_kernel(tab_ref, tiles_ref, sb_ref, *, blk):
    r = lax.broadcasted_iota(I32, (blk, blk), 0)
    c = lax.broadcasted_iota(I32, (blk, blk), 1)
    for t in range(2):
        dist = c - r + t * blk
        bucket = _t5_bucket(jnp.maximum(dist, 0))
        for h in range(ATT_HEADS):
            far = tab_ref[N_BUCKETS - 1, h]
            val = jnp.zeros((blk, blk), F32)
            for b in range(N_BUCKETS - 1):
                val = jnp.where(bucket == b, (tab_ref[b, h] - far) * LOG2E, val)
            if t == 0:
                val = jnp.where(dist >= 0, val, NEG_INF)
            tiles_ref[h, t] = val
    page_w = ATT_HEADS * PAGE_SIZE
    sr = lax.broadcasted_iota(I32, (2 * ATT_HEADS, page_w + LANES), 0)
    sc = lax.broadcasted_iota(I32, (2 * ATT_HEADS, page_w + LANES), 1)
    sbucket = _t5_bucket(jnp.where(sc < page_w, PAGE_SIZE - (sc >> 2), 0))
    sval = jnp.zeros(sr.shape, F32)
    for h in range(ATT_HEADS):
        far = tab_ref[N_BUCKETS - 1, h]
        hval = jnp.zeros(sr.shape, F32)
        for b in range(N_BUCKETS - 1):
            hval = jnp.where(sbucket == b, (tab_ref[b, h] - far) * LOG2E, hval)
        sval = jnp.where((sr >> 1) == h, hval, sval)
    sb_ref[...] = sval


def _bias_tiles(rel_bias, blk):
    return pl.pallas_call(
        functools.partial(_bias_kernel, blk=blk),
        out_shape=(jax.ShapeDtypeStruct((ATT_HEADS, 2, blk, blk), F32),
                   jax.ShapeDtypeStruct((2 * ATT_HEADS, ATT_HEADS * PAGE_SIZE + LANES), F32)),
        in_specs=[pl.BlockSpec(memory_space=pltpu.SMEM)],
        name="t5_bias_tiles",
    )(rel_bias)


def _proj_kernel(x_ref, g_ref, w_ref, zr_ref, q_ref, k_ref, v_ref, kb_ref, vt_ref, gate_ref):
    h = _rms(x_ref[...], g_ref[...], NORM_EPS).astype(BF16)
    o = 0
    zr_ref[...] = _dot(h, w_ref[:, o:o + RWKV_PROJ]); o += RWKV_PROJ
    q_ref[...] = _dot(h, w_ref[:, o:o + ATT_QK_W]); o += ATT_QK_W
    k = _dot(h, w_ref[:, o:o + ATT_QK_W]); o += ATT_QK_W
    v = _dot(h, w_ref[:, o:o + ATT_V_W]); o += ATT_V_W
    tm = k.shape[0]
    for hd in range(ATT_HEADS):
        k_ref[pl.ds(hd, tm, stride=ATT_HEADS), :] = k[:, hd * LANES:(hd + 1) * LANES]
        v_ref[pl.ds(hd, tm, stride=ATT_HEADS), :] = v[:, hd * LANES:(hd + 1) * LANES]
    kb_ref[...] = k.astype(BF16)
    vt_ref[...] = v.T.astype(BF16)
    gate_ref[...] = jax.nn.sigmoid(_dot(h, w_ref[:, o:o + 2 * D_MODEL]))


def _proj(x, g, w_bf16):
    t = x.shape[0]
    tm = min(256, t)
    n_total = w_bf16.shape[1]
    row = lambda n: pl.BlockSpec((tm, n), lambda i: (i, 0))
    sds = lambda n, dt=F32: jax.ShapeDtypeStruct((t, n), dt)
    heads = pl.BlockSpec((ATT_HEADS * tm, LANES), lambda i: (i, 0))
    heads_sds = jax.ShapeDtypeStruct((ATT_HEADS * t, LANES), F32)
    return pl.pallas_call(
        _proj_kernel,
        grid=(t // tm,),
        in_specs=[row(D_MODEL), _full((1, D_MODEL)), _full((D_MODEL, n_total))],
        out_specs=(row(RWKV_PROJ), row(ATT_QK_W), heads, heads,
                   row(ATT_QK_W), pl.BlockSpec((ATT_V_W, tm), lambda i: (0, i)), row(2 * D_MODEL)),
        out_shape=(sds(RWKV_PROJ), sds(ATT_QK_W), heads_sds, heads_sds,
                   sds(ATT_QK_W, BF16), jax.ShapeDtypeStruct((ATT_V_W, t), BF16), sds(2 * D_MODEL)),
        compiler_params=_params(("parallel",)),
        name="in_proj",
    )(x, g, w_bf16)


def _rwkv_pre_kernel(*refs, per_token_prev):
    if per_token_prev:
        z_ref, zp_ref = refs[:2]
        rest = refs[2:]
    else:
        z_ref, halo_ref, s0_ref = refs[:3]
        rest = refs[3:]
    (mu_ref, w0_ref, a0_ref, kk_ref, ka_ref, w2_ref, a2_ref, g2_ref, seg_ref,
     r_out, lw_out, kx_out, v_out, kk_out, kka_out, g_out) = rest
    z = z_ref[...]
    if per_token_prev:
        zp = zp_ref[...]
    else:
        first = jnp.where(pl.program_id(0) == 0, s0_ref[...], halo_ref[7:8, :])
        rowid = lax.broadcasted_iota(I32, z.shape, 0)
        zp = jnp.where(rowid == 0, first, pltpu.roll(z, 1, axis=0))
    zs = z + (zp - z) * mu_ref[...]
    r = zs[:, 0:RWKV_W]
    k = zs[:, RWKV_W:2 * RWKV_W]
    v = zs[:, 2 * RWKV_W:3 * RWKV_W]
    wa = zs[:, 3 * RWKV_W:3 * RWKV_W + DECAY_LORA + AAA_LORA]
    g_lo = zs[:, 3 * RWKV_W + DECAY_LORA + AAA_LORA:]
    wpre = w0_ref[...] + _mm(jnp.tanh(wa), w2_ref[...], passes=3)
    nw = -wpre
    softplus = jnp.maximum(nw, 0.0) + jnp.log1p(jnp.exp(-jnp.abs(nw)))
    w = -softplus - 0.5
    a = jax.nn.sigmoid(a0_ref[...] + _mm(wa, a2_ref[...], passes=3))
    g = _mm(jax.nn.sigmoid(g_lo), g2_ref[...], passes=3)
    kk = k * kk_ref[...]
    nrm = jnp.sqrt(_seg_sum(kk * kk, seg_ref[...]))
    kk = kk / jnp.maximum(nrm, 1e-12)
    r_out[...] = r
    lw_out[...] = -jnp.exp(w)
    kx_out[...] = k * (1.0 + (a - 1.0) * ka_ref[...])
    v_out[...] = v
    kk_out[...] = kk
    kka_out[...] = kk * a
    g_out[...] = g


def _rwkv_pre(z, zprev_or_shift0, per_token_prev, wts):
    t = z.shape[0]
    tm = min(256, t)
    row = lambda n: pl.BlockSpec((tm, n), lambda i: (i, 0))
    vec = lambda n: _full((1, n))
    if per_token_prev:
        lead = [row(RWKV_PROJ), row(RWKV_PROJ)]
        args = [z, zprev_or_shift0]
    else:
        halo = pl.BlockSpec((8, RWKV_PROJ), lambda i: (jnp.maximum(i * (tm // 8) - 1, 0), 0))
        lead = [row(RWKV_PROJ), halo, vec(RWKV_PROJ)]
        args = [z, z, zprev_or_shift0]
    lora_w = DECAY_LORA + AAA_LORA
    return pl.pallas_call(
        functools.partial(_rwkv_pre_kernel, per_token_prev=per_token_prev),
        grid=(t // tm,),
        in_specs=lead + [vec(RWKV_PROJ), vec(RWKV_W), vec(RWKV_W), vec(RWKV_W), vec(RWKV_W),
                         _full((lora_w, RWKV_W)), _full((lora_w, RWKV_W)), _full((GATE_LORA, RWKV_W)),
                         _full((RWKV_W, RWKV_W))],
        out_specs=tuple(row(RWKV_W) for _ in range(7)),
        out_shape=tuple(jax.ShapeDtypeStruct((t, RWKV_W), F32) for _ in range(7)),
        compiler_params=_params(("parallel",)),
        name="rwkv_pre",
    )(*args, wts['mu'], wts['w0'], wts['a0'], wts['k_k'], wts['k_a'], wts['w2p'], wts['a2p'], wts['g2'],
      wts['seg64'])


def _stack(x, lane_head):
    return jnp.concatenate([jnp.where(lane_head == h, x, 0.0) for h in range(4)], axis=0)


def _parts(x, passes):
    return tuple(_split(x, 2)) if passes == 3 else (x.astype(BF16),)


def _mmp(pa, pb, dims=NN):
    if len(pa) == 1:
        return _dot(pa[0], pb[0], dims)
    return _dot(pa[0], pb[0], dims) + (_dot(pa[0], pb[1], dims) + _dot(pa[1], pb[0], dims))


def _rwkv_chunk_kernel(r_ref, lw_ref, kx_ref, v_ref, kk_ref, ka_ref, tri_ref, y_ref, sout_ref, s_scr,
                       *, nsub, passes):
    @pl.when(pl.program_id(0) == 0)
    def _():
        s_scr[...] = jnp.zeros_like(s_scr)

    L = CHUNK
    n4 = 4 * L
    nquad = RWKV_W // QUAD
    tri = tri_ref[...]
    lane_head = lax.broadcasted_iota(I32, (L, QUAD), 1) // RWKV_HEAD
    rr = lax.broadcasted_iota(I32, (n4, n4), 0)
    cc = lax.broadcasted_iota(I32, (n4, n4), 1)
    strict = (rr % L) > (cc % L)
    incl = (rr % L) >= (cc % L)
    eye = jnp.where(rr == cc, 1.0, 0.0)
    parts = functools.partial(_parts, passes=passes)
    insts = [(c, qd) for c in range(nsub) for qd in range(nquad)]

    st = {}
    for c, qd in insts:
        rows = slice(c * L, (c + 1) * L)
        cols = slice(qd * QUAD, (qd + 1) * QUAD)
        lw, ka, kx = lw_ref[rows, cols], ka_ref[rows, cols], kx_ref[rows, cols]
        cs = _mm_exact(tri, lw)
        cs_end = cs[L - 1:L, :]
        e_neg = jnp.exp(-cs)
        e_end = jnp.exp(cs_end - cs)
        stk = lambda x: parts(_stack(x, lane_head))
        p_a = stk(-kk_ref[rows, cols] * jnp.exp(cs - lw))
        p_b = stk(ka * e_neg)
        p_k = stk(kx * e_neg)
        p_r = stk(r_ref[rows, cols] * jnp.exp(cs))
        p_v = stk(v_ref[rows, cols])
        st[c, qd] = dict(
            p_a=p_a, p_r=p_r, p_v=p_v, p_bh=stk(ka * e_end), p_kh=stk(kx * e_end), decay=jnp.exp(cs_end),
            n_ab=jnp.where(strict, _mmp(p_a, p_b, NT), 0.0), n_ak=jnp.where(strict, _mmp(p_a, p_k, NT), 0.0),
            n_rb=jnp.where(incl, _mmp(p_r, p_b, NT), 0.0), n_rk=jnp.where(incl, _mmp(p_r, p_k, NT), 0.0))

    xs = {k: d['n_ab'] for k, d in st.items()}
    tinv = {k: eye + d['n_ab'] for k, d in st.items()}
    for _ in range(5):
        for k in insts:
            px = parts(xs[k])
            xs[k] = _mmp(px, px)
            tinv[k] = tinv[k] + _mmp(parts(tinv[k]), parts(xs[k]))

    for k in insts:
        d = st[k]
        p_t = parts(tinv[k])
        d['p_wa'] = parts(_mmp(p_t, d['p_a']))
        d['xu'] = _mmp(p_t, parts(_mmp(parts(d['n_ak']), d['p_v'])))
        d['yk'] = _mmp(parts(d['n_rk']), d['p_v'])
        d['hk'] = _mmp(d['p_v'], d['p_kh'], TN)
        d['p_rb'] = parts(d['n_rb'])

    for qd in range(nquad):
        s = s_scr[qd]
        for c in range(nsub):
            d = st[c, qd]
            p_s = parts(s)
            p_u = parts(_mmp(d['p_wa'], p_s, NT) + d['xu'])
            s_y = _mmp(d['p_r'], p_s, NT) + _mmp(d['p_rb'], p_u) + d['yk']
            y = jnp.zeros((L, QUAD), F32)
            for h in range(4):
                y = y + jnp.where(lane_head == h, s_y[h * L:(h + 1) * L, :], 0.0)
            y_ref[c * L:(c + 1) * L, qd * QUAD:(qd + 1) * QUAD] = y
            s = s * d['decay'] + _mmp(p_u, d['p_bh'], TN) + d['hk']
        s_scr[qd] = s

    @pl.when(pl.program_id(0) == pl.num_programs(0) - 1)
    def _():
        sout_ref[...] = s_scr[...]


def _rwkv_chunked(r, lw, kx, v, kk, ka, passes=RWKV_PASSES):
    t = r.shape[0]
    nsub = next(n for n in (2, 1) if t % (n * CHUNK) == 0)
    lb = nsub * CHUNK
    nq = RWKV_W // QUAD
    tri = (jnp.arange(CHUNK)[:, None] >= jnp.arange(CHUNK)[None, :]).astype(BF16)
    row = pl.BlockSpec((lb, RWKV_W), lambda i: (i, 0))
    y, s_bd = pl.pallas_call(
        functools.partial(_rwkv_chunk_kernel, nsub=nsub, passes=passes),
        grid=(t // lb,),
        in_specs=[row] * 6 + [_full((CHUNK, CHUNK))],
        out_specs=(row, _full((nq, QUAD, QUAD))),
        out_shape=(jax.ShapeDtypeStruct((t, RWKV_W), F32), jax.ShapeDtypeStruct((nq, QUAD, QUAD), F32)),
        scratch_shapes=[pltpu.VMEM((nq, QUAD, QUAD), F32)],
        compiler_params=_params(("arbitrary",)),
        name="rwkv_chunked",
    )(r, lw, kx, v, kk, ka, tri)
    blocks = [s_bd[h // 4, (h % 4) * RWKV_HEAD:(h % 4 + 1) * RWKV_HEAD, (h % 4) * RWKV_HEAD:(h % 4 + 1) * RWKV_HEAD]
              for h in range(RWKV_HEADS)]
    return y, jnp.stack(blocks)


def _rwkv_step_kernel(s_ref, r_ref, lw_ref, kx_ref, kk_ref, ka_ref, v_ref, sout_ref, y_ref):
    s = s_ref[...]
    sa = -jnp.sum(s * kk_ref[...][None], axis=1, keepdims=True)
    s_new = s * jnp.exp(lw_ref[...])[None] + sa * ka_ref[...][None] + v_ref[...][:, None, :] * kx_ref[...][None]
    sout_ref[...] = s_new
    y_ref[...] = jnp.sum(s_new * r_ref[...][None], axis=1)


def _rwkv_step(state, r, lw, kx, kk, ka, v):
    db = state.shape[0]
    vec = lambda a: a.T.reshape(RWKV_HEADS, RWKV_HEAD, db)
    st = pl.BlockSpec((None, RWKV_HEAD, RWKV_HEAD, db), lambda h: (h, 0, 0, 0))
    vs = pl.BlockSpec((None, RWKV_HEAD, db), lambda h: (h, 0, 0))
    s_new, y = pl.pallas_call(
        _rwkv_step_kernel,
        grid=(RWKV_HEADS,),
        in_specs=[st, vs, vs, vs, vs, vs, vs],
        out_specs=(st, vs),
        out_shape=(jax.ShapeDtypeStruct((RWKV_HEADS, RWKV_HEAD, RWKV_HEAD, db), F32),
                   jax.ShapeDtypeStruct((RWKV_HEADS, RWKV_HEAD, db), F32)),
        compiler_params=_params(("parallel",)),
        name="rwkv_step",
    )(state.transpose(1, 2, 3, 0), vec(r), vec(lw), vec(kx), vec(kk), vec(ka), vec(v))
    return y.reshape(RWKV_W, db).T, s_new.transpose(3, 0, 1, 2)


def _attn_prompt_kernel(lam_ref, q_ref, k_ref, vt_ref, bias_ref, o_ref, m_ref, l_ref, acc_ref, s_ref, *, blk):
    i = pl.program_id(1)
    qt = q_ref[...].T * (HEAD_QK ** -0.5 * LOG2E)
    sub = lax.broadcasted_iota(I32, qt.shape, 0)
    qt2 = jnp.concatenate([jnp.where(sub < HEAD_QK, qt, 0.0), jnp.where(sub >= HEAD_QK, qt, 0.0)],
                          axis=1).astype(BF16)
    m_ref[...] = jnp.full_like(m_ref, -jnp.inf)
    l_ref[...] = jnp.zeros_like(l_ref)
    acc_ref[...] = jnp.zeros_like(acc_ref)

    def scores(j, slot):
        start = pl.multiple_of(j * blk, blk)
        s_ref[slot] = _dot(k_ref[pl.ds(start, blk), :], qt2)

    def step(j, slot, bias, prefetch):
        if prefetch:
            scores(j + 1, 1 - slot)
        s = s_ref[slot]
        if bias is not None:
            s = s + jnp.concatenate([bias, bias], axis=1)
        m_old = m_ref[...]
        m_new = jnp.maximum(m_old, jnp.max(s, axis=0, keepdims=True))
        alpha = jnp.exp2(m_old - m_new)
        p = jnp.exp2(s - m_new)
        l_ref[...] = alpha * l_ref[...] + jnp.sum(p, axis=0, keepdims=True)
        start = pl.multiple_of(j * blk, blk)
        acc_ref[...] = alpha * acc_ref[...] + _dot(vt_ref[:, pl.ds(start, blk)], p.astype(BF16))
        m_ref[...] = m_new

    n_far = jnp.maximum(i - 1, 0)
    scores(0, 0)

    def pair_body(jj, carry):
        step(2 * jj, 0, None, True)
        step(2 * jj + 1, 1, None, True)
        return carry

    lax.fori_loop(0, n_far // 2, pair_body, 0)
    odd = (n_far % 2) == 1

    @pl.when(i == 0)
    def _():
        step(i, 0, bias_ref[0], False)

    @pl.when((i >= 1) & jnp.logical_not(odd))
    def _():
        step(i - 1, 0, bias_ref[1], True)
        step(i, 1, bias_ref[0], False)

    @pl.when((i >= 1) & odd)
    def _():
        step(i - 2, 0, None, True)
        step(i - 1, 1, bias_ref[1], True)
        step(i, 0, bias_ref[0], False)

    o = acc_ref[...] / l_ref[...]
    o_ref[...] = (o[:, :blk] - lam_ref[0] * o[:, blk:]).T


def _attn_prompt(lam, q, kb, vt, tiles, blk):
    t = q.shape[0]
    return pl.pallas_call(
        functools.partial(_attn_prompt_kernel, blk=blk),
        grid=(ATT_HEADS, t // blk),
        in_specs=[pl.BlockSpec(memory_space=pltpu.SMEM),
                  pl.BlockSpec((blk, LANES), lambda h, i: (i, h)),
                  pl.BlockSpec((t, LANES), lambda h, i: (0, h)),
                  pl.BlockSpec((HEAD_V, t), lambda h, i: (h, 0)),
                  pl.BlockSpec((None, 2, blk, blk), lambda h, i: (h, 0, 0, 0))],
        out_specs=pl.BlockSpec((blk, LANES), lambda h, i: (i, h)),
        out_shape=jax.ShapeDtypeStruct((t, ATT_V_W), F32),
        scratch_shapes=[pltpu.VMEM((1, 2 * blk), F32), pltpu.VMEM((1, 2 * blk), F32),
                        pltpu.VMEM((HEAD_V, 2 * blk), F32), pltpu.VMEM((2, blk, 2 * blk), F32)],
        compiler_params=_params(("parallel", "parallel")),
        name="attn_prompt",
    )(lam, q, kb, vt, tiles)


def _attn_sample_kernel(pt_ref, lam_ref, q_ref, kn_ref, vn_ref, sb_ref, *rest, pages_per_step):
    del pt_ref
    pp = pages_per_step
    page_w = ATT_HEADS * PAGE_SIZE
    k_refs = rest[:pp]
    v_refs = rest[pp:2 * pp]
    o_ref, m_ref, l_ref, acc_ref = rest[2 * pp:]
    g = pl.program_id(1)
    last = g == pl.num_programs(1) - 1
    row = lax.broadcasted_iota(I32, (2 * ATT_HEADS, LANES), 0)
    lane = lax.broadcasted_iota(I32, (2 * ATT_HEADS, LANES), 1)

    def rows8(x4):
        out = jnp.zeros((2 * ATT_HEADS, LANES), F32)
        for h in range(ATT_HEADS):
            out = jnp.where((row >> 1) == h, x4[h:h + 1, :], out)
        return out

    map_lanes = (row & 1) == (lane >= HEAD_QK).astype(I32)
    q8 = jnp.where(map_lanes, rows8(q_ref[0]) * (HEAD_QK ** -0.5 * LOG2E), 0.0)

    @pl.when(g == 0)
    def _():
        s_self = jnp.sum(q8 * rows8(kn_ref[0]), axis=-1, keepdims=True) + sb_ref[:, page_w:page_w + 1]
        m_ref[...] = s_self
        l_ref[...] = jnp.ones_like(l_ref)
        acc_ref[...] = rows8(vn_ref[0])

    qb = q8.astype(BF16)
    ss = [_dot(qb, k_refs[p][...].astype(BF16), NT) for p in range(pp)]
    ss[-1] = ss[-1] + jnp.where(last, sb_ref[:, :page_w], 0.0)
    s_all = jnp.concatenate(ss, axis=-1)
    col_head = lax.broadcasted_iota(I32, s_all.shape, 1) & (ATT_HEADS - 1)
    row_head = lax.broadcasted_iota(I32, s_all.shape, 0) >> 1
    s_all = jnp.where(col_head == row_head, s_all, -jnp.inf)
    m_old = m_ref[...]
    m_new = jnp.maximum(m_old, jnp.max(s_all, axis=-1, keepdims=True))
    alpha = jnp.exp2(m_old - m_new)
    p_all = jnp.exp2(s_all - m_new).astype(BF16)
    l_ref[...] = alpha * l_ref[...] + jnp.sum(p_all.astype(F32), axis=-1, keepdims=True)
    acc = alpha * acc_ref[...]
    for p in range(pp):
        acc = acc + _dot(p_all[:, p * page_w:(p + 1) * page_w], v_refs[p][...].astype(BF16))
    acc_ref[...] = acc
    m_ref[...] = m_new

    @pl.when(last)
    def _():
        o = acc_ref[...] / l_ref[...]
        o_ref[0] = jnp.concatenate([o[2 * h:2 * h + 1, :] - lam_ref[0] * o[2 * h + 1:2 * h + 2, :]
                                    for h in range(ATT_HEADS)], axis=0)


def _attn_sample(lam, q, k_new, v_new, cache_k, cache_v, layer, page_table, sbias):
    db, n_pages = page_table.shape
    pp = next(p for p in (16, 8, 1) if n_pages % p == 0)
    page_w = ATT_HEADS * PAGE_SIZE
    tok = pl.BlockSpec((1, ATT_HEADS, LANES), lambda b, g, pt: (b, 0, 0))

    def page_spec(p):
        return pl.BlockSpec((None, None, page_w, LANES), lambda b, g, pt: (layer, pt[b, g * pp + p], 0, 0))

    pages = lambda c: c.reshape(c.shape[0], c.shape[1], page_w, LANES)
    shp = (db, ATT_HEADS, LANES)
    out = pl.pallas_call(
        functools.partial(_attn_sample_kernel, pages_per_step=pp),
        grid_spec=pltpu.PrefetchScalarGridSpec(
            num_scalar_prefetch=1,
            grid=(db, n_pages // pp),
            in_specs=[pl.BlockSpec(memory_space=pltpu.SMEM), tok, tok, tok, _full(sbias.shape)]
                     + [page_spec(p) for p in range(pp)] * 2,
            out_specs=tok,
            scratch_shapes=[pltpu.VMEM((2 * ATT_HEADS, 1), F32), pltpu.VMEM((2 * ATT_HEADS, 1), F32),
                            pltpu.VMEM((2 * ATT_HEADS, LANES), F32)]),
        out_shape=jax.ShapeDtypeStruct(shp, F32),
        compiler_params=_params(("parallel", "arbitrary")),
        name="attn_sample",
    )(page_table, lam, q.reshape(shp), k_new.reshape(shp), v_new.reshape(shp), sbias,
      *([pages(cache_k)] * pp), *([pages(cache_v)] * pp))
    return out.reshape(db, ATT_V_W)


def _merge_kernel(x_ref, y_ref, r_ref, kx_ref, v_ref, g_ref, o_ref, gate_ref,
                  lnw_ref, lnb_ref, rk_ref, sub_ref, seg_ref, wbr_ref, wba_ref, wout_ref, nffn_ref,
                  x1_ref, xnt_ref, *, attn_scale):
    seg = seg_ref[...]
    y = y_ref[...]
    inv_n = 1.0 / RWKV_HEAD
    mean = _seg_sum(y, seg) * inv_n
    yc = y - mean
    var = _seg_sum(yc * yc, seg) * inv_n
    yn = yc * lax.rsqrt(var + RWKV_LN_EPS) * lnw_ref[...] + lnb_ref[...]
    v = v_ref[...]
    bonus = _seg_sum(r_ref[...] * kx_ref[...] * rk_ref[...], seg) * v
    y_rwkv = (yn + bonus) * g_ref[...]
    o = o_ref[...]
    parts = []
    for h in range(ATT_HEADS):
        oh = o[:, h * HEAD_V:(h + 1) * HEAD_V]
        parts.append(_rms(oh, sub_ref[...], SUBLN_EPS) * attn_scale)
    y_attn = jnp.concatenate(parts, axis=-1)
    gates = gate_ref[...]
    merged = (gates[:, :D_MODEL] * _mm(y_rwkv, wbr_ref[...])
              + gates[:, D_MODEL:] * _mm(y_attn, wba_ref[...]))
    x1 = x_ref[...] + _mm(merged, wout_ref[...])
    x1_ref[...] = x1
    xnt_ref[...] = _rms(x1, nffn_ref[...], NORM_EPS).T.astype(BF16)


def _merge(x, y, r, kx, v, g, o, gates, wts, attn_scale):
    t = x.shape[0]
    tm = min(256, t)
    row = lambda n: pl.BlockSpec((tm, n), lambda i: (i, 0))
    vec = lambda n: _full((1, n))
    return pl.pallas_call(
        functools.partial(_merge_kernel, attn_scale=attn_scale),
        grid=(t // tm,),
        in_specs=[row(D_MODEL)] + [row(RWKV_W)] * 6 + [row(2 * D_MODEL),
                  vec(RWKV_W), vec(RWKV_W), vec(RWKV_W), vec(HEAD_V), _full((RWKV_W, RWKV_W)),
                  _full((RWKV_W, D_MODEL)), _full((ATT_V_W, D_MODEL)), _full((D_MODEL, D_MODEL)),
                  vec(D_MODEL)],
        out_specs=(row(D_MODEL), pl.BlockSpec((D_MODEL, tm), lambda i: (0, i))),
        out_shape=(jax.ShapeDtypeStruct((t, D_MODEL), F32), jax.ShapeDtypeStruct((D_MODEL, t), BF16)),
        compiler_params=_params(("parallel",)),
        name="merge",
    )(x, y, r, kx, v, g, o, gates, wts['ln_w'], wts['ln_b'], wts['r_k'], wts['subln'], wts['seg64'],
      wts['w_br_rwkv'], wts['w_br_attn'], wts['w_out'], wts['norm_ffn'])


def _oddeven_merge(lo, hi, r):
    step = r * 2
    if step < hi - lo:
        yield from _oddeven_merge(lo, hi, step)
        yield from _oddeven_merge(lo + r, hi, step)
        yield from [(i, i + r) for i in range(lo + r, hi - r, step)]
    else:
        yield (lo, lo + r)


def _oddeven_sort(lo, hi):
    if hi - lo >= 1:
        mid = lo + (hi - lo) // 2
        yield from _oddeven_sort(lo, mid)
        yield from _oddeven_sort(mid + 1, hi)
        yield from _oddeven_merge(lo, hi, 1)


SUBLANES = 8
_SORT16 = tuple(_oddeven_sort(0, 15))


def _top_sorted(vals, k):
    ngroups = vals.shape[0] // SUBLANES
    lv = [vals[g * SUBLANES:(g + 1) * SUBLANES] for g in range(ngroups)] + [None] * (16 - ngroups)
    for i, j in _SORT16:
        a, b = lv[i], lv[j]
        if b is None:
            continue
        if a is None:
            lv[i], lv[j] = b, None
        else:
            lv[i], lv[j] = jnp.maximum(a, b), jnp.minimum(a, b)
    lv = lv[:k]
    ninf = jnp.full(lv[0].shape, -jnp.inf, F32)
    rows = []
    for it in range(k):
        mx = jnp.max(lv[0], axis=0, keepdims=True)
        rows.append(mx)
        hit = lv[0] == mx
        for l in range(k - it - 1):
            if lv[l] is None:
                break
            nxt = lv[l + 1] if (l + 1 < len(lv) and lv[l + 1] is not None) else ninf
            lv[l] = jnp.where(hit, nxt, lv[l])
    return rows


def _peer_route_kernel(xnt_ref, wq_ref, k1_ref, k2_ref, rank_ref, nsel_ref, e1_ref, e2_ref):
    half = PEER_TOPK // 2
    qt = _dot(wq_ref[...], xnt_ref[...])
    k1 = k1_ref[...].astype(BF16)
    k2 = k2_ref[...].astype(BF16)
    for h in range(PEER_HEADS):
        base = h * PEER_QDIM
        s1 = _dot(k1, qt[base:base + PEER_HALF, :].astype(BF16))
        s2 = _dot(k2, qt[base + PEER_HALF:base + PEER_QDIM, :].astype(BF16))
        v1 = _top_sorted(s1, PEER_TOPK)
        v2 = _top_sorted(s2, PEER_TOPK)
        v1m = jnp.concatenate(v1, axis=0)
        v2m = jnp.concatenate(v2, axis=0)
        groups = ([v1[0] + v2m] + [v1[i] + v2m[:half] for i in range(1, half)] + [v1m[half:] + v2[0]])
        cand = jnp.concatenate(groups, axis=0)
        tau = _top_sorted(cand, PEER_TOPK)[-1]
        top = v1[0] + v2[0]
        z = jnp.sum(jnp.where(cand >= tau, jnp.exp(cand - top), 0.0), axis=0, keepdims=True)
        count = lambda grp: jnp.sum(jnp.where(grp >= tau, 1.0, 0.0), axis=0, keepdims=True)
        n_rank = [count(groups[i]) for i in range(half)]
        tail = jnp.where(groups[half] >= tau, 1.0, 0.0)
        n_rank += [tail[i:i + 1, :] for i in range(half)]
        nsel = jnp.zeros(s1.shape, F32)
        rank2 = jnp.full(s2.shape, float(PEER_TOPK), F32)
        for i in range(PEER_TOPK):
            nsel = jnp.where(s1 == v1[i], n_rank[i], nsel)
            rank2 = jnp.where(s2 == v2[i], float(i), rank2)
        rank_ref[h] = rank2.astype(BF16)
        nsel_ref[h] = nsel
        e1_ref[h] = jnp.exp(s1 - v1[0]) / z
        e2_ref[h] = jnp.exp(s2 - v2[0]).astype(BF16)


def _peer_tokens(t):
    return min(512, t)


def _peer_route(xnt, wts):
    t = xnt.shape[1]
    tb = _peer_tokens(t)
    col = pl.BlockSpec((PEER_HEADS, N_KEYS, tb), lambda i: (0, 0, i))
    big = lambda dt: jax.ShapeDtypeStruct((PEER_HEADS, N_KEYS, t), dt)
    return pl.pallas_call(
        _peer_route_kernel,
        grid=(t // tb,),
        in_specs=[pl.BlockSpec((D_MODEL, tb), lambda i: (0, i)),
                  _full((PEER_HEADS * PEER_QDIM, D_MODEL)),
                  _full((N_KEYS, PEER_HALF)), _full((N_KEYS, PEER_HALF))],
        out_specs=(col, col, col, col),
        out_shape=(big(BF16), big(F32), big(F32), big(BF16)),
        compiler_params=_params(("parallel",)),
        name="peer_route",
    )(xnt, wts['wqT'], wts['peer_k1'], wts['peer_k2'])


ROW_TILE = 16
PEER_EB = 1024


def _peer_dense_kernel(xnt_ref, rank_ref, nsel_ref, e1_ref, e2_ref, eu_ref, evt_ref, x1_ref,
                       pe_ref, wple_ref, nple_ref, wgate_ref, nfin_ref,
                       o_ref, yt_ref, u_ref, act_ref, gated_ref, bn_ref, be1_ref, *, eb, final_norm):
    j = pl.program_id(1)

    @pl.when(j == 0)
    def _():
        yt_ref[...] = jnp.zeros_like(yt_ref)
        act_ref[...] = jnp.zeros_like(act_ref)

    xnt = xnt_ref[...]
    u_ref[...] = _dot(eu_ref[...], xnt)
    groups = eb // N_KEYS
    tb = xnt.shape[1]
    zero = jnp.zeros((ROW_TILE, tb), BF16)
    live = jnp.where(j >= 1, 1.0, 0.0)
    prev = jnp.maximum(j - 1, 0)
    for a in range(groups):
        i1 = prev * groups + a
        for h in range(PEER_HEADS):
            bn_ref[a, h] = jnp.broadcast_to(nsel_ref[h, pl.ds(i1, 1), :], (ROW_TILE, tb)).astype(BF16)
            be1_ref[a, h] = jnp.broadcast_to(e1_ref[h, pl.ds(i1, 1), :] * live, (ROW_TILE, tb)).astype(BF16)
        for rt in range(N_KEYS // ROW_TILE):
            rows = slice(rt * ROW_TILE, (rt + 1) * ROW_TILE)
            w = None
            for h in range(PEER_HEADS):
                wh = jnp.where(rank_ref[h, rows, :] < bn_ref[a, h], e2_ref[h, rows, :], zero) * be1_ref[a, h]
                w = wh if w is None else w + wh
            erows = slice(a * N_KEYS + rt * ROW_TILE, a * N_KEYS + (rt + 1) * ROW_TILE)
            gated_ref[erows, :] = w * act_ref[erows, :]
    yt_ref[...] += _dot(evt_ref[...], gated_ref[...])
    u = u_ref[...]
    act_ref[...] = (0.5 * u * (1.0 + lax.erf(u * (2.0 ** -0.5)))).astype(BF16)

    @pl.when(j == pl.num_programs(1) - 1)
    def _():
        x = x1_ref[...] + yt_ref[...].T
        gate = jax.nn.sigmoid(_mm(_rms(x, nple_ref[...], NORM_EPS), wgate_ref[...]))
        x = x + _mm(pe_ref[...], wple_ref[...]) * gate
        o_ref[...] = _rms(x, nfin_ref[...], NORM_EPS) if final_norm else x


def _peer_dense(xnt, route, x1, pe, wts, final_norm):
    t = x1.shape[0]
    tb = _peer_tokens(t)
    eb = PEER_EB
    nblk = N_EXPERTS // eb
    col = pl.BlockSpec((PEER_HEADS, N_KEYS, tb), lambda i, j: (0, 0, i))
    return pl.pallas_call(
        functools.partial(_peer_dense_kernel, eb=eb, final_norm=final_norm),
        grid=(t // tb, nblk + 1),
        in_specs=[pl.BlockSpec((D_MODEL, tb), lambda i, j: (0, i)), col, col, col, col,
                  pl.BlockSpec((eb, D_MODEL), lambda i, j: (jnp.minimum(j, nblk - 1), 0)),
                  pl.BlockSpec((None, D_MODEL, eb), lambda i, j: (jnp.maximum(j - 1, 0), 0, 0)),
                  pl.BlockSpec((tb, D_MODEL), lambda i, j: (i, 0)),
                  pl.BlockSpec((tb, PLE_DIM), lambda i, j: (i, 0)),
                  _full((PLE_DIM, D_MODEL)), _full((1, D_MODEL)), _full((D_MODEL, D_MODEL)),
                  _full((1, D_MODEL))],
        out_specs=pl.BlockSpec((tb, D_MODEL), lambda i, j: (i, 0)),
        out_shape=jax.ShapeDtypeStruct((t, D_MODEL), F32),
        scratch_shapes=[pltpu.VMEM((D_MODEL, tb), F32), pltpu.VMEM((eb, tb), F32),
                        pltpu.VMEM((eb, tb), BF16), pltpu.VMEM((eb, tb), BF16),
                        pltpu.VMEM((eb // N_KEYS, PEER_HEADS, ROW_TILE, tb), BF16),
                        pltpu.VMEM((eb // N_KEYS, PEER_HEADS, ROW_TILE, tb), BF16)],
        compiler_params=_params(("parallel", "arbitrary")),
        name="peer_dense",
    )(xnt, *route, wts['eu'], wts['evT'], x1, pe,
      wts['w_ple'], wts['norm_ple'], wts['w_ple_gate'], wts['norm_final'])


def _token_stages(x, pe, y_rwkv_raw, pre, o_attn, gates, wts, attn_scale, final_norm):
    r, _, kx, v, _, _, g = pre
    x1, xnt = _merge(x, y_rwkv_raw, r, kx, v, g, o_attn, gates, wts, attn_scale)
    return _peer_dense(xnt, _peer_route(xnt, wts), x1, pe, wts, final_norm)


def kernel(x_prompt, x_sample, cache_k, cache_v, state_wkv, state_shift, page_table, p_prompt, p_sample,
           norm_mix, w_in, rwkv_mu, rwkv_w0, rwkv_w2, rwkv_a0, rwkv_a2, rwkv_g2, rwkv_k_k, rwkv_k_a,
           rwkv_r_k, rwkv_ln_w, rwkv_ln_b, attn_lq1, attn_lk1, attn_lq2, attn_lk2, attn_subln, rel_bias,
           w_br_rwkv, w_br_attn, w_out, norm_ffn, peer_wq, peer_k1, peer_k2, peer_u, peer_v,
           norm_ple, w_ple, w_ple_gate, norm_final):
    depth = w_in.shape[0]
    bsz, seq = x_prompt.shape[0], x_prompt.shape[1]
    db, dseq = x_sample.shape[0], x_sample.shape[1]
    assert bsz == 1 and dseq == 1, "prompt batch and decode length are fixed at 1"
    blk = next((b for b in (512, 256) if seq % b == 0), seq)
    xp = x_prompt.reshape(seq, D_MODEL)
    xs = x_sample.reshape(db, D_MODEL)
    tiles, sbias = _bias_tiles(rel_bias, blk)
    seg64 = (jnp.arange(RWKV_W)[:, None] // RWKV_HEAD == jnp.arange(RWKV_W)[None, :] // RWKV_HEAD).astype(BF16)
    row = lambda a: a.reshape(1, -1)
    outs = [[] for _ in range(8)]
    for l in range(depth):
        zpad = jnp.zeros((DECAY_LORA, RWKV_W), F32)
        wts = {
            'mu': row(rwkv_mu[l]), 'w0': row(rwkv_w0[l]), 'a0': row(rwkv_a0[l]),
            'k_k': row(rwkv_k_k[l]), 'k_a': row(rwkv_k_a[l]),
            'w2p': jnp.concatenate([rwkv_w2[l], zpad], axis=0),
            'a2p': jnp.concatenate([zpad, rwkv_a2[l]], axis=0),
            'g2': rwkv_g2[l], 'seg64': seg64,
            'ln_w': row(rwkv_ln_w[l]), 'ln_b': row(rwkv_ln_b[l]), 'r_k': row(rwkv_r_k[l]),
            'subln': row(attn_subln[l]),
            'w_br_rwkv': w_br_rwkv[l].astype(BF16), 'w_br_attn': w_br_attn[l].astype(BF16),
            'w_out': w_out[l].astype(BF16), 'norm_ffn': row(norm_ffn[l]),
            'wqT': peer_wq[l].T.astype(BF16),
            'peer_k1': peer_k1[l], 'peer_k2': peer_k2[l],
            'eu': peer_u[l].astype(BF16),
            'evT': peer_v[l].reshape(N_EXPERTS // PEER_EB, PEER_EB, D_MODEL).transpose(0, 2, 1).astype(BF16),
            'w_ple': w_ple[l].astype(BF16), 'norm_ple': row(norm_ple[l]),
            'w_ple_gate': w_ple_gate[l].astype(BF16), 'norm_final': row(norm_final),
        }
        lam_init = 0.8 - 0.6 * math.exp(-0.3 * l)
        lam = (jnp.exp(jnp.sum(attn_lq1[l] * attn_lk1[l])) - jnp.exp(jnp.sum(attn_lq2[l] * attn_lk2[l]))
               + lam_init).reshape(1).astype(F32)
        w_in_b = w_in[l].astype(BF16)
        g_mix = row(norm_mix[l])

        zr, q, k, v, kb, vt, gates = _proj(xp, g_mix, w_in_b)
        pre = _rwkv_pre(zr, jnp.zeros((1, RWKV_PROJ), F32), False, wts)
        y_raw, wkv_p = _rwkv_chunked(*pre[:6])
        o = _attn_prompt(lam, q, kb, vt, tiles, blk)
        xp = _token_stages(xp, p_prompt[l].reshape(seq, PLE_DIM), y_raw, pre, o, gates, wts,
                           1.0 - lam_init, l == depth - 1)
        outs[0].append(k.reshape(1, seq, ATT_HEADS, 2 * HEAD_QK))
        outs[1].append(v.reshape(1, seq, ATT_HEADS, HEAD_V))
        outs[4].append(wkv_p[None])
        outs[6].append(zr[seq - 1:seq, :])

        zr, q, k, v, _, _, gates = _proj(xs, g_mix, w_in_b)
        pre = _rwkv_pre(zr, state_shift[l], True, wts)
        r, lw, kx, vv, kk, ka, _ = pre
        y_raw, wkv_s = _rwkv_step(state_wkv[l], r, lw, kx, kk, ka, vv)
        o = _attn_sample(lam, q, k, v, cache_k, cache_v, l, page_table, sbias)
        xs = _token_stages(xs, p_sample[l].reshape(db, PLE_DIM), y_raw, pre, o, gates, wts,
                           1.0 - lam_init, l == depth - 1)
        outs[2].append(k.reshape(db, 1, ATT_HEADS, 2 * HEAD_QK))
        outs[3].append(v.reshape(db, 1, ATT_HEADS, HEAD_V))
        outs[5].append(wkv_s)
        outs[7].append(zr)
    y_prompt = xp.reshape(1, seq, D_MODEL)
    y_sample = xs.reshape(db, 1, D_MODEL)
    st = lambda i: jnp.stack(outs[i])
    return (y_prompt, y_sample, st(0), st(1), st(2), st(3), st(4), st(5), st(6), st(7))
```

```python
import functools
import math

import jax
import jax.numpy as jnp
from jax import lax
from jax.experimental import pallas as pl
from jax.experimental.pallas import tpu as pltpu

F32 = jnp.float32
BF16 = jnp.bfloat16
I32 = jnp.int32

D_MODEL = 1024
PAGE_SIZE = 128
RWKV_HEADS = 8
RWKV_HEAD = 64
RWKV_W = RWKV_HEADS * RWKV_HEAD
DECAY_LORA = 64
AAA_LORA = 64
GATE_LORA = 128
RWKV_PROJ = 3 * RWKV_W + DECAY_LORA + AAA_LORA + GATE_LORA
RWKV_LN_EPS = 64e-5
ATT_HEADS = 4
HEAD_QK = 64
HEAD_V = 2 * HEAD_QK
ATT_QK_W = ATT_HEADS * 2 * HEAD_QK
ATT_V_W = ATT_HEADS * HEAD_V
SUBLN_EPS = 1e-5
N_BUCKETS = 32
MAX_EXACT = N_BUCKETS // 2
MAX_DISTANCE = 128
PEER_HEADS = 8
PEER_QDIM = 256
PEER_HALF = PEER_QDIM // 2
N_KEYS = 128
N_EXPERTS = N_KEYS * N_KEYS
PEER_TOPK = 16
PLE_DIM = 256
NORM_EPS = 1e-6
NEG_INF = -1e30

LANES = 128
QUAD = 4 * RWKV_HEAD
CHUNK = 64
VMEM_LIMIT = 56 * 1024 * 1024
LOG2E = math.log2(math.e)
RWKV_PASSES = 1

NN = (((1,), (0,)), ((), ()))
NT = (((1,), (1,)), ((), ()))
TN = (((0,), (0,)), ((), ()))


def _dot(a, b, dims=NN):
    return lax.dot_general(a, b, dims, preferred_element_type=F32)


def _split(x, n):
    parts = []
    for _ in range(n):
        p = x.astype(BF16)
        parts.append(p)
        x = x - p.astype(F32)
    return parts


def _mm(a, b, dims=NN, passes=1):
    if passes == 1:
        return _dot(a.astype(BF16), b.astype(BF16), dims)
    ah, al = _split(a, 2)
    bh, bl = _split(b, 2)
    return _dot(ah, bh, dims) + (_dot(ah, bl, dims) + _dot(al, bh, dims))


def _mm_exact(a, b, dims=NN):
    a1, a2, a3 = _split(a, 3)
    return _dot(a1, b, dims) + (_dot(a2, b, dims) + _dot(a3, b, dims))


def _rms(x, g, eps):
    return x * lax.rsqrt(jnp.mean(x * x, axis=-1, keepdims=True) + eps) * g


def _full(shape):
    return pl.BlockSpec(shape, lambda *_: (0,) * len(shape))


def _params(sem):
    return pltpu.CompilerParams(dimension_semantics=sem, vmem_limit_bytes=VMEM_LIMIT)


def _t5_bucket(n):
    nf = jnp.maximum(n, 1).astype(F32)
    large = MAX_EXACT + (jnp.log(nf / MAX_EXACT) / math.log(MAX_DISTANCE / MAX_EXACT)
                         * (N_BUCKETS - MAX_EXACT)).astype(I32)
    large = jnp.minimum(large, N_BUCKETS - 1)
    return jnp.where(n < MAX_EXACT, n, large)


def _bias_kernel(tab_ref, tiles_ref, sb_ref, *, blk):
    r = lax.broadcasted_iota(I32, (blk, blk), 0)
    c = lax.broadcasted_iota(I32, (blk, blk), 1)
    for t in range(2):
        dist = c - r + t * blk
        bucket = _t5_bucket(jnp.maximum(dist, 0))
        for h in range(ATT_HEADS):
            far = tab_ref[N_BUCKETS - 1, h]
            val = jnp.zeros((blk, blk), F32)
            for b in range(N_BUCKETS - 1):
                val = jnp.where(bucket == b, (tab_ref[b, h] - far) * LOG2E, val)
            if t == 0:
                val = jnp.where(dist >= 0, val, NEG_INF)
            tiles_ref[h, t] = val
    page_w = ATT_HEADS * PAGE_SIZE
    sr = lax.broadcasted_iota(I32, (2 * ATT_HEADS, page_w + LANES), 0)
    sc = lax.broadcasted_iota(I32, (2 * ATT_HEADS, page_w + LANES), 1)
    sbucket = _t5_bucket(jnp.where(sc < page_w, PAGE_SIZE - (sc >> 2), 0))
    sval = jnp.zeros(sr.shape, F32)
    for h in range(ATT_HEADS):
        far = tab_ref[N_BUCKETS - 1, h]
        hval = jnp.zeros(sr.shape, F32)
        for b in range(N_BUCKETS - 1):
            hval = jnp.where(sbucket == b, (tab_ref[b, h] - far) * LOG2E, hval)
        sval = jnp.where((sr >> 1) == h, hval, sval)
    sb_ref[...] = sval


def _bias_tiles(rel_bias, blk):
    return pl.pallas_call(
        functools.partial(_bias_kernel, blk=blk),
        out_shape=(jax.ShapeDtypeStruct((ATT_HEADS, 2, blk, blk), F32),
                   jax.ShapeDtypeStruct((2 * ATT_HEADS, ATT_HEADS * PAGE_SIZE + LANES), F32)),
        in_specs=[pl.BlockSpec(memory_space=pltpu.SMEM)],
        name="t5_bias_tiles",
    )(rel_bias)


def _proj_kernel(x_ref, g_ref, w_ref, zr_ref, q_ref, k_ref, v_ref, kb_ref, vt_ref, gate_ref):
    h = _rms(x_ref[...], g_ref[...], NORM_EPS).astype(BF16)
    o = 0
    zr_ref[...] = _dot(h, w_ref[:, o:o + RWKV_PROJ]); o += RWKV_PROJ
    q_ref[...] = _dot(h, w_ref[:, o:o + ATT_QK_W]); o += ATT_QK_W
    k = _dot(h, w_ref[:, o:o + ATT_QK_W]); o += ATT_QK_W
    v = _dot(h, w_ref[:, o:o + ATT_V_W]); o += ATT_V_W
    tm = k.shape[0]
    for hd in range(ATT_HEADS):
        k_ref[pl.ds(hd, tm, stride=ATT_HEADS), :] = k[:, hd * LANES:(hd + 1) * LANES]
        v_ref[pl.ds(hd, tm, stride=ATT_HEADS), :] = v[:, hd * LANES:(hd + 1) * LANES]
    kb_ref[...] = k.astype(BF16)
    vt_ref[...] = v.T.astype(BF16)
    gate_ref[...] = jax.nn.sigmoid(_dot(h, w_ref[:, o:o + 2 * D_MODEL]))


def _proj(x, g, w_bf16):
    t = x.shape[0]
    tm = min(256, t)
    n_total = w_bf16.shape[1]
    row = lambda n: pl.BlockSpec((tm, n), lambda i: (i, 0))
    sds = lambda n, dt=F32: jax.ShapeDtypeStruct((t, n), dt)
    heads = pl.BlockSpec((ATT_HEADS * tm, LANES), lambda i: (i, 0))
    heads_sds = jax.ShapeDtypeStruct((ATT_HEADS * t, LANES), F32)
    return pl.pallas_call(
        _proj_kernel,
        grid=(t // tm,),
        in_specs=[row(D_MODEL), _full((1, D_MODEL)), _full((D_MODEL, n_total))],
        out_specs=(row(RWKV_PROJ), row(ATT_QK_W), heads, heads,
                   row(ATT_QK_W), pl.BlockSpec((ATT_V_W, tm), lambda i: (0, i)), row(2 * D_MODEL)),
        out_shape=(sds(RWKV_PROJ), sds(ATT_QK_W), heads_sds, heads_sds,
                   sds(ATT_QK_W, BF16), jax.ShapeDtypeStruct((ATT_V_W, t), BF16), sds(2 * D_MODEL)),
        compiler_params=_params(("parallel",)),
        name="in_proj",
    )(x, g, w_bf16)


def _rwkv_pre_kernel(*refs, per_token_prev):
    if per_token_prev:
        z_ref, zp_ref = refs[:2]
        rest = refs[2:]
    else:
        z_ref, halo_ref, s0_ref = refs[:3]
        rest = refs[3:]
    (mu_ref, w0_ref, a0_ref, kk_ref, ka_ref, w2_ref, a2_ref, g2_ref, seg_ref,
     r_out, lw_out, kx_out, v_out, kk_out, kka_out, g_out) = rest
    z = z_ref[...]
    if per_token_prev:
        zp = zp_ref[...]
    else:
        first = jnp.where(pl.program_id(0) == 0, s0_ref[...], halo_ref[7:8, :])
        rowid = lax.broadcasted_iota(I32, z.shape, 0)
        zp = jnp.where(rowid == 0, first, pltpu.roll(z, 1, axis=0))
    zs = z + (zp - z) * mu_ref[...]
    r = zs[:, 0:RWKV_W]
    k = zs[:, RWKV_W:2 * RWKV_W]
    v = zs[:, 2 * RWKV_W:3 * RWKV_W]
    wa = zs[:, 3 * RWKV_W:3 * RWKV_W + DECAY_LORA + AAA_LORA]
    g_lo = zs[:, 3 * RWKV_W + DECAY_LORA + AAA_LORA:]
    wpre = w0_ref[...] + _mm(jnp.tanh(wa), w2_ref[...], passes=3)
    nw = -wpre
    softplus = jnp.maximum(nw, 0.0) + jnp.log1p(jnp.exp(-jnp.abs(nw)))
    w = -softplus - 0.5
    a = jax.nn.sigmoid(a0_ref[...] + _mm(wa, a2_ref[...], passes=3))
    g = _mm(jax.nn.sigmoid(g_lo), g2_ref[...], passes=3)
    kk = k * kk_ref[...]
    nrm = jnp.sqrt(_mm_exact(kk * kk, seg_ref[...]))
    kk = kk / jnp.maximum(nrm, 1e-12)
    r_out[...] = r
    lw_out[...] = -jnp.exp(w)
    kx_out[...] = k * (1.0 + (a - 1.0) * ka_ref[...])
    v_out[...] = v
    kk_out[...] = kk
    kka_out[...] = kk * a
    g_out[...] = g


def _rwkv_pre(z, zprev_or_shift0, per_token_prev, wts):
    t = z.shape[0]
    tm = min(256, t)
    row = lambda n: pl.BlockSpec((tm, n), lambda i: (i, 0))
    vec = lambda n: _full((1, n))
    if per_token_prev:
        lead = [row(RWKV_PROJ), row(RWKV_PROJ)]
        args = [z, zprev_or_shift0]
    else:
        halo = pl.BlockSpec((8, RWKV_PROJ), lambda i: (jnp.maximum(i * (tm // 8) - 1, 0), 0))
        lead = [row(RWKV_PROJ), halo, vec(RWKV_PROJ)]
        args = [z, z, zprev_or_shift0]
    lora_w = DECAY_LORA + AAA_LORA
    return pl.pallas_call(
        functools.partial(_rwkv_pre_kernel, per_token_prev=per_token_prev),
        grid=(t // tm,),
        in_specs=lead + [vec(RWKV_PROJ), vec(RWKV_W), vec(RWKV_W), vec(RWKV_W), vec(RWKV_W),
                         _full((lora_w, RWKV_W)), _full((lora_w, RWKV_W)), _full((GATE_LORA, RWKV_W)),
                         _full((RWKV_W, RWKV_W))],
        out_specs=tuple(row(RWKV_W) for _ in range(7)),
        out_shape=tuple(jax.ShapeDtypeStruct((t, RWKV_W), F32) for _ in range(7)),
        compiler_params=_params(("parallel",)),
        name="rwkv_pre",
    )(*args, wts['mu'], wts['w0'], wts['a0'], wts['k_k'], wts['k_a'], wts['w2p'], wts['a2p'], wts['g2'],
      wts['seg64'])


def _stack(x, lane_head):
    return jnp.concatenate([jnp.where(lane_head == h, x, 0.0) for h in range(4)], axis=0)


def _parts(x, passes):
    return tuple(_split(x, 2)) if passes == 3 else (x.astype(BF16),)


def _mmp(pa, pb, dims=NN):
    if len(pa) == 1:
        return _dot(pa[0], pb[0], dims)
    return _dot(pa[0], pb[0], dims) + (_dot(pa[0], pb[1], dims) + _dot(pa[1], pb[0], dims))


def _rwkv_chunk_kernel(r_ref, lw_ref, kx_ref, v_ref, kk_ref, ka_ref, tri_ref, y_ref, sout_ref, s_scr,
                       *, nsub, passes):
    @pl.when(pl.program_id(0) == 0)
    def _():
        s_scr[...] = jnp.zeros_like(s_scr)

    L = CHUNK
    n4 = 4 * L
    nquad = RWKV_W // QUAD
    tri = tri_ref[...]
    lane_head = lax.broadcasted_iota(I32, (L, QUAD), 1) // RWKV_HEAD
    rr = lax.broadcasted_iota(I32, (n4, n4), 0)
    cc = lax.broadcasted_iota(I32, (n4, n4), 1)
    strict = (rr % L) > (cc % L)
    incl = (rr % L) >= (cc % L)
    eye = jnp.where(rr == cc, 1.0, 0.0)
    parts = functools.partial(_parts, passes=passes)
    insts = [(c, qd) for c in range(nsub) for qd in range(nquad)]

    st = {}
    for c, qd in insts:
        rows = slice(c * L, (c + 1) * L)
        cols = slice(qd * QUAD, (qd + 1) * QUAD)
        lw, ka, kx = lw_ref[rows, cols], ka_ref[rows, cols], kx_ref[rows, cols]
        cs = _mm_exact(tri, lw)
        cs_end = cs[L - 1:L, :]
        e_neg = jnp.exp(-cs)
        e_end = jnp.exp(cs_end - cs)
        stk = lambda x: parts(_stack(x, lane_head))
        p_a = stk(-kk_ref[rows, cols] * jnp.exp(cs - lw))
        p_b = stk(ka * e_neg)
        p_k = stk(kx * e_neg)
        p_r = stk(r_ref[rows, cols] * jnp.exp(cs))
        p_v = stk(v_ref[rows, cols])
        st[c, qd] = dict(
            p_a=p_a, p_r=p_r, p_v=p_v, p_bh=stk(ka * e_end), p_kh=stk(kx * e_end), decay=jnp.exp(cs_end),
            n_ab=jnp.where(strict, _mmp(p_a, p_b, NT), 0.0), n_ak=jnp.where(strict, _mmp(p_a, p_k, NT), 0.0),
            n_rb=jnp.where(incl, _mmp(p_r, p_b, NT), 0.0), n_rk=jnp.where(incl, _mmp(p_r, p_k, NT), 0.0))

    xs = {k: d['n_ab'] for k, d in st.items()}
    tinv = {k: eye + d['n_ab'] for k, d in st.items()}
    for _ in range(5):
        for k in insts:
            px = parts(xs[k])
            xs[k] = _mmp(px, px)
            tinv[k] = tinv[k] + _mmp(parts(tinv[k]), parts(xs[k]))

    for k in insts:
        d = st[k]
        p_t = parts(tinv[k])
        d['p_wa'] = parts(_mmp(p_t, d['p_a']))
        d['xu'] = _mmp(p_t, parts(_mmp(parts(d['n_ak']), d['p_v'])))
        d['yk'] = _mmp(parts(d['n_rk']), d['p_v'])
        d['hk'] = _mmp(d['p_v'], d['p_kh'], TN)
        d['p_rb'] = parts(d['n_rb'])

    for qd in range(nquad):
        s = s_scr[qd]
        for c in range(nsub):
            d = st[c, qd]
            p_s = parts(s)
            p_u = parts(_mmp(d['p_wa'], p_s, NT) + d['xu'])
            s_y = _mmp(d['p_r'], p_s, NT) + _mmp(d['p_rb'], p_u) + d['yk']
            y = jnp.zeros((L, QUAD), F32)
            for h in range(4):
                y = y + jnp.where(lane_head == h, s_y[h * L:(h + 1) * L, :], 0.0)
            y_ref[c * L:(c + 1) * L, qd * QUAD:(qd + 1) * QUAD] = y
            s = s * d['decay'] + _mmp(p_u, d['p_bh'], TN) + d['hk']
        s_scr[qd] = s

    @pl.when(pl.program_id(0) == pl.num_programs(0) - 1)
    def _():
        sout_ref[...] = s_scr[...]


def _rwkv_chunked(r, lw, kx, v, kk, ka, passes=RWKV_PASSES):
    t = r.shape[0]
    nsub = next(n for n in (2, 1) if t % (n * CHUNK) == 0)
    lb = nsub * CHUNK
    nq = RWKV_W // QUAD
    tri = (jnp.arange(CHUNK)[:, None] >= jnp.arange(CHUNK)[None, :]).astype(BF16)
    row = pl.BlockSpec((lb, RWKV_W), lambda i: (i, 0))
    y, s_bd = pl.pallas_call(
        functools.partial(_rwkv_chunk_kernel, nsub=nsub, passes=passes),
        grid=(t // lb,),
        in_specs=[row] * 6 + [_full((CHUNK, CHUNK))],
        out_specs=(row, _full((nq, QUAD, QUAD))),
        out_shape=(jax.ShapeDtypeStruct((t, RWKV_W), F32), jax.ShapeDtypeStruct((nq, QUAD, QUAD), F32)),
        scratch_shapes=[pltpu.VMEM((nq, QUAD, QUAD), F32)],
        compiler_params=_params(("arbitrary",)),
        name="rwkv_chunked",
    )(r, lw, kx, v, kk, ka, tri)
    blocks = [s_bd[h // 4, (h % 4) * RWKV_HEAD:(h % 4 + 1) * RWKV_HEAD, (h % 4) * RWKV_HEAD:(h % 4 + 1) * RWKV_HEAD]
              for h in range(RWKV_HEADS)]
    return y, jnp.stack(blocks)


def _rwkv_step_kernel(s_ref, r_ref, lw_ref, kx_ref, kk_ref, ka_ref, v_ref, sout_ref, y_ref):
    s = s_ref[...]
    sa = -jnp.sum(s * kk_ref[...][None], axis=1, keepdims=True)
    s_new = s * jnp.exp(lw_ref[...])[None] + sa * ka_ref[...][None] + v_ref[...][:, None, :] * kx_ref[...][None]
    sout_ref[...] = s_new
    y_ref[...] = jnp.sum(s_new * r_ref[...][None], axis=1)


def _rwkv_step(state, r, lw, kx, kk, ka, v):
    db = state.shape[0]
    vec = lambda a: a.T.reshape(RWKV_HEADS, RWKV_HEAD, db)
    st = pl.BlockSpec((None, RWKV_HEAD, RWKV_HEAD, db), lambda h: (h, 0, 0, 0))
    vs = pl.BlockSpec((None, RWKV_HEAD, db), lambda h: (h, 0, 0))
    s_new, y = pl.pallas_call(
        _rwkv_step_kernel,
        grid=(RWKV_HEADS,),
        in_specs=[st, vs, vs, vs, vs, vs, vs],
        out_specs=(st, vs),
        out_shape=(jax.ShapeDtypeStruct((RWKV_HEADS, RWKV_HEAD, RWKV_HEAD, db), F32),
                   jax.ShapeDtypeStruct((RWKV_HEADS, RWKV_HEAD, db), F32)),
        compiler_params=_params(("parallel",)),
        name="rwkv_step",
    )(state.transpose(1, 2, 3, 0), vec(r), vec(lw), vec(kx), vec(kk), vec(ka), vec(v))
    return y.reshape(RWKV_W, db).T, s_new.transpose(3, 0, 1, 2)


COL_CHUNK = 256


def _attn_prompt_kernel(lam_ref, q_ref, k_ref, vt_ref, bias_ref, o_ref, m_ref, l_ref, acc_ref, s_ref, *, blk):
    i = pl.program_id(1)
    qt = q_ref[...].T * (HEAD_QK ** -0.5 * LOG2E)
    sub = lax.broadcasted_iota(I32, qt.shape, 0)
    qt2 = jnp.concatenate([jnp.where(sub < HEAD_QK, qt, 0.0), jnp.where(sub >= HEAD_QK, qt, 0.0)],
                          axis=1).astype(BF16)
    m_ref[...] = jnp.full_like(m_ref, -jnp.inf)
    l_ref[...] = jnp.zeros_like(l_ref)
    acc_ref[...] = jnp.zeros_like(acc_ref)

    def scores(j, slot):
        start = pl.multiple_of(j * blk, blk)
        s_ref[slot] = _dot(k_ref[pl.ds(start, blk), :], qt2)

    def step(j, slot, bias, prefetch):
        if prefetch:
            scores(j + 1, 1 - slot)
        start = pl.multiple_of(j * blk, blk)
        vt_blk = vt_ref[:, pl.ds(start, blk)]
        for c0 in range(0, 2 * blk, COL_CHUNK):
            cols = slice(c0, c0 + COL_CHUNK)
            s = s_ref[slot, :, cols]
            if bias is not None:
                s = s + bias[:, c0 % blk:c0 % blk + COL_CHUNK]
            m_old = m_ref[:, cols]
            m_new = jnp.maximum(m_old, jnp.max(s, axis=0, keepdims=True))
            alpha = jnp.exp2(m_old - m_new)
            p = jnp.exp2(s - m_new)
            l_ref[:, cols] = alpha * l_ref[:, cols] + jnp.sum(p, axis=0, keepdims=True)
            acc_ref[:, cols] = alpha * acc_ref[:, cols] + _dot(vt_blk, p.astype(BF16))
            m_ref[:, cols] = m_new

    n_far = jnp.maximum(i - 1, 0)
    scores(0, 0)

    def pair_body(jj, carry):
        step(2 * jj, 0, None, True)
        step(2 * jj + 1, 1, None, True)
        return carry

    lax.fori_loop(0, n_far // 2, pair_body, 0)
    odd = (n_far % 2) == 1

    @pl.when(i == 0)
    def _():
        step(i, 0, bias_ref[0], False)

    @pl.when((i >= 1) & jnp.logical_not(odd))
    def _():
        step(i - 1, 0, bias_ref[1], True)
        step(i, 1, bias_ref[0], False)

    @pl.when((i >= 1) & odd)
    def _():
        step(i - 2, 0, None, True)
        step(i - 1, 1, bias_ref[1], True)
        step(i, 0, bias_ref[0], False)

    o = acc_ref[...] / l_ref[...]
    o_ref[...] = (o[:, :blk] - lam_ref[0] * o[:, blk:]).T


def _attn_prompt(lam, q, kb, vt, tiles, blk):
    t = q.shape[0]
    return pl.pallas_call(
        functools.partial(_attn_prompt_kernel, blk=blk),
        grid=(ATT_HEADS, t // blk),
        in_specs=[pl.BlockSpec(memory_space=pltpu.SMEM),
                  pl.BlockSpec((blk, LANES), lambda h, i: (i, h)),
                  pl.BlockSpec((t, LANES), lambda h, i: (0, h)),
                  pl.BlockSpec((HEAD_V, t), lambda h, i: (h, 0)),
                  pl.BlockSpec((None, 2, blk, blk), lambda h, i: (h, 0, 0, 0))],
        out_specs=pl.BlockSpec((blk, LANES), lambda h, i: (i, h)),
        out_shape=jax.ShapeDtypeStruct((t, ATT_V_W), F32),
        scratch_shapes=[pltpu.VMEM((1, 2 * blk), F32), pltpu.VMEM((1, 2 * blk), F32),
                        pltpu.VMEM((HEAD_V, 2 * blk), F32), pltpu.VMEM((2, blk, 2 * blk), F32)],
        compiler_params=_params(("parallel", "parallel")),
        name="attn_prompt",
    )(lam, q, kb, vt, tiles)


def _attn_sample_kernel(pt_ref, lam_ref, q_ref, kn_ref, vn_ref, sb_ref, *rest, pages_per_step):
    del pt_ref
    pp = pages_per_step
    page_w = ATT_HEADS * PAGE_SIZE
    k_refs = rest[:pp]
    v_refs = rest[pp:2 * pp]
    o_ref, m_ref, l_ref, acc_ref = rest[2 * pp:]
    g = pl.program_id(1)
    last = g == pl.num_programs(1) - 1
    row = lax.broadcasted_iota(I32, (2 * ATT_HEADS, LANES), 0)
    lane = lax.broadcasted_iota(I32, (2 * ATT_HEADS, LANES), 1)

    def rows8(x4):
        out = jnp.zeros((2 * ATT_HEADS, LANES), F32)
        for h in range(ATT_HEADS):
            out = jnp.where((row >> 1) == h, x4[h:h + 1, :], out)
        return out

    map_lanes = (row & 1) == (lane >= HEAD_QK).astype(I32)
    q8 = jnp.where(map_lanes, rows8(q_ref[0]) * (HEAD_QK ** -0.5 * LOG2E), 0.0)

    @pl.when(g == 0)
    def _():
        s_self = jnp.sum(q8 * rows8(kn_ref[0]), axis=-1, keepdims=True) + sb_ref[:, page_w:page_w + 1]
        m_ref[...] = s_self
        l_ref[...] = jnp.ones_like(l_ref)
        acc_ref[...] = rows8(vn_ref[0])

    qb = q8.astype(BF16)
    ss = [_dot(qb, k_refs[p][...].astype(BF16), NT) for p in range(pp)]
    ss[-1] = ss[-1] + jnp.where(last, sb_ref[:, :page_w], 0.0)
    s_all = jnp.concatenate(ss, axis=-1)
    col_head = lax.broadcasted_iota(I32, s_all.shape, 1) & (ATT_HEADS - 1)
    row_head = lax.broadcasted_iota(I32, s_all.shape, 0) >> 1
    s_all = jnp.where(col_head == row_head, s_all, -jnp.inf)
    m_old = m_ref[...]
    m_new = jnp.maximum(m_old, jnp.max(s_all, axis=-1, keepdims=True))
    alpha = jnp.exp2(m_old - m_new)
    p_all = jnp.exp2(s_all - m_new).astype(BF16)
    l_ref[...] = alpha * l_ref[...] + jnp.sum(p_all.astype(F32), axis=-1, keepdims=True)
    acc = alpha * acc_ref[...]
    for p in range(pp):
        acc = acc + _dot(p_all[:, p * page_w:(p + 1) * page_w], v_refs[p][...].astype(BF16))
    acc_ref[...] = acc
    m_ref[...] = m_new

    @pl.when(last)
    def _():
        o = acc_ref[...] / l_ref[...]
        o_ref[0] = jnp.concatenate([o[2 * h:2 * h + 1, :] - lam_ref[0] * o[2 * h + 1:2 * h + 2, :]
                                    for h in range(ATT_HEADS)], axis=0)


def _attn_sample(lam, q, k_new, v_new, cache_k, cache_v, layer, page_table, sbias):
    db, n_pages = page_table.shape
    pp = next(p for p in (16, 8, 1) if n_pages % p == 0)
    page_w = ATT_HEADS * PAGE_SIZE
    tok = pl.BlockSpec((1, ATT_HEADS, LANES), lambda b, g, pt: (b, 0, 0))

    def page_spec(p):
        return pl.BlockSpec((None, None, page_w, LANES), lambda b, g, pt: (layer, pt[b, g * pp + p], 0, 0))

    pages = lambda c: c.reshape(c.shape[0], c.shape[1], page_w, LANES)
    shp = (db, ATT_HEADS, LANES)
    out = pl.pallas_call(
        functools.partial(_attn_sample_kernel, pages_per_step=pp),
        grid_spec=pltpu.PrefetchScalarGridSpec(
            num_scalar_prefetch=1,
            grid=(db, n_pages // pp),
            in_specs=[pl.BlockSpec(memory_space=pltpu.SMEM), tok, tok, tok, _full(sbias.shape)]
                     + [page_spec(p) for p in range(pp)] * 2,
            out_specs=tok,
            scratch_shapes=[pltpu.VMEM((2 * ATT_HEADS, 1), F32), pltpu.VMEM((2 * ATT_HEADS, 1), F32),
                            pltpu.VMEM((2 * ATT_HEADS, LANES), F32)]),
        out_shape=jax.ShapeDtypeStruct(shp, F32),
        compiler_params=_params(("parallel", "arbitrary")),
        name="attn_sample",
    )(page_table, lam, q.reshape(shp), k_new.reshape(shp), v_new.reshape(shp), sbias,
      *([pages(cache_k)] * pp), *([pages(cache_v)] * pp))
    return out.reshape(db, ATT_V_W)


def _merge_kernel(x_ref, y_ref, r_ref, kx_ref, v_ref, g_ref, o_ref, gate_ref,
                  lnw_ref, lnb_ref, rk_ref, sub_ref, seg_ref, wbr_ref, wba_ref, wout_ref, nffn_ref,
                  x1_ref, xnt_ref, *, attn_scale):
    seg = seg_ref[...]
    y = y_ref[...]
    inv_n = 1.0 / RWKV_HEAD
    mean = _mm_exact(y, seg) * inv_n
    yc = y - mean
    var = _mm_exact(yc * yc, seg) * inv_n
    yn = yc * lax.rsqrt(var + RWKV_LN_EPS) * lnw_ref[...] + lnb_ref[...]
    v = v_ref[...]
    bonus = _mm_exact(r_ref[...] * kx_ref[...] * rk_ref[...], seg) * v
    y_rwkv = (yn + bonus) * g_ref[...]
    o = o_ref[...]
    parts = []
    for h in range(ATT_HEADS):
        oh = o[:, h * HEAD_V:(h + 1) * HEAD_V]
        parts.append(_rms(oh, sub_ref[...], SUBLN_EPS) * attn_scale)
    y_attn = jnp.concatenate(parts, axis=-1)
    gates = gate_ref[...]
    merged = (gates[:, :D_MODEL] * _mm(y_rwkv, wbr_ref[...])
              + gates[:, D_MODEL:] * _mm(y_attn, wba_ref[...]))
    x1 = x_ref[...] + _mm(merged, wout_ref[...])
    x1_ref[...] = x1
    xnt_ref[...] = _rms(x1, nffn_ref[...], NORM_EPS).T.astype(BF16)


def _merge(x, y, r, kx, v, g, o, gates, wts, attn_scale):
    t = x.shape[0]
    tm = min(256, t)
    row = lambda n: pl.BlockSpec((tm, n), lambda i: (i, 0))
    vec = lambda n: _full((1, n))
    return pl.pallas_call(
        functools.partial(_merge_kernel, attn_scale=attn_scale),
        grid=(t // tm,),
        in_specs=[row(D_MODEL)] + [row(RWKV_W)] * 6 + [row(2 * D_MODEL),
                  vec(RWKV_W), vec(RWKV_W), vec(RWKV_W), vec(HEAD_V), _full((RWKV_W, RWKV_W)),
                  _full((RWKV_W, D_MODEL)), _full((ATT_V_W, D_MODEL)), _full((D_MODEL, D_MODEL)),
                  vec(D_MODEL)],
        out_specs=(row(D_MODEL), pl.BlockSpec((D_MODEL, tm), lambda i: (0, i))),
        out_shape=(jax.ShapeDtypeStruct((t, D_MODEL), F32), jax.ShapeDtypeStruct((D_MODEL, t), BF16)),
        compiler_params=_params(("parallel",)),
        name="merge",
    )(x, y, r, kx, v, g, o, gates, wts['ln_w'], wts['ln_b'], wts['r_k'], wts['subln'], wts['seg64'],
      wts['w_br_rwkv'], wts['w_br_attn'], wts['w_out'], wts['norm_ffn'])


def _oddeven_merge(lo, hi, r):
    step = r * 2
    if step < hi - lo:
        yield from _oddeven_merge(lo, hi, step)
        yield from _oddeven_merge(lo + r, hi, step)
        yield from [(i, i + r) for i in range(lo + r, hi - r, step)]
    else:
        yield (lo, lo + r)


def _oddeven_sort(lo, hi):
    if hi - lo >= 1:
        mid = lo + (hi - lo) // 2
        yield from _oddeven_sort(lo, mid)
        yield from _oddeven_sort(mid + 1, hi)
        yield from _oddeven_merge(lo, hi, 1)


SUBLANES = 8
_SORT16 = tuple(_oddeven_sort(0, 15))


def _top_sorted(vals, k):
    ngroups = vals.shape[0] // SUBLANES
    lv = [vals[g * SUBLANES:(g + 1) * SUBLANES] for g in range(ngroups)] + [None] * (16 - ngroups)
    for i, j in _SORT16:
        a, b = lv[i], lv[j]
        if b is None:
            continue
        if a is None:
            lv[i], lv[j] = b, None
        else:
            lv[i], lv[j] = jnp.maximum(a, b), jnp.minimum(a, b)
    lv = lv[:k]
    ninf = jnp.full(lv[0].shape, -jnp.inf, F32)
    rows = []
    for it in range(k):
        mx = jnp.max(lv[0], axis=0, keepdims=True)
        rows.append(mx)
        hit = lv[0] == mx
        for l in range(k - it - 1):
            if lv[l] is None:
                break
            nxt = lv[l + 1] if (l + 1 < len(lv) and lv[l + 1] is not None) else ninf
            lv[l] = jnp.where(hit, nxt, lv[l])
    return rows


def _peer_route_kernel(xnt_ref, wq_ref, k1_ref, k2_ref, rank_ref, nsel_ref, e1_ref, e2_ref):
    half = PEER_TOPK // 2
    qt = _dot(wq_ref[...], xnt_ref[...])
    k1 = k1_ref[...].astype(BF16)
    k2 = k2_ref[...].astype(BF16)
    for h in range(PEER_HEADS):
        base = h * PEER_QDIM
        s1 = _dot(k1, qt[base:base + PEER_HALF, :].astype(BF16))
        s2 = _dot(k2, qt[base + PEER_HALF:base + PEER_QDIM, :].astype(BF16))
        v1 = _top_sorted(s1, PEER_TOPK)
        v2 = _top_sorted(s2, PEER_TOPK)
        v1m = jnp.concatenate(v1, axis=0)
        v2m = jnp.concatenate(v2, axis=0)
        groups = ([v1[0] + v2m] + [v1[i] + v2m[:half] for i in range(1, half)] + [v1m[half:] + v2[0]])
        cand = jnp.concatenate(groups, axis=0)
        tau = _top_sorted(cand, PEER_TOPK)[-1]
        top = v1[0] + v2[0]
        z = jnp.sum(jnp.where(cand >= tau, jnp.exp(cand - top), 0.0), axis=0, keepdims=True)
        count = lambda grp: jnp.sum(jnp.where(grp >= tau, 1.0, 0.0), axis=0, keepdims=True)
        n_rank = [count(groups[i]) for i in range(half)]
        tail = jnp.where(groups[half] >= tau, 1.0, 0.0)
        n_rank += [tail[i:i + 1, :] for i in range(half)]
        nsel = jnp.zeros(s1.shape, F32)
        rank2 = jnp.full(s2.shape, float(PEER_TOPK), F32)
        for i in range(PEER_TOPK):
            nsel = jnp.where(s1 == v1[i], n_rank[i], nsel)
            rank2 = jnp.where(s2 == v2[i], float(i), rank2)
        rank_ref[h] = rank2.astype(BF16)
        nsel_ref[h] = nsel
        e1_ref[h] = jnp.exp(s1 - v1[0]) / z
        e2_ref[h] = jnp.exp(s2 - v2[0]).astype(BF16)


def _peer_tokens(t):
    return min(512, t)


def _peer_route(xnt, wts):
    t = xnt.shape[1]
    tb = _peer_tokens(t)
    col = pl.BlockSpec((PEER_HEADS, N_KEYS, tb), lambda i: (0, 0, i))
    big = lambda dt: jax.ShapeDtypeStruct((PEER_HEADS, N_KEYS, t), dt)
    return pl.pallas_call(
        _peer_route_kernel,
        grid=(t // tb,),
        in_specs=[pl.BlockSpec((D_MODEL, tb), lambda i: (0, i)),
                  _full((PEER_HEADS * PEER_QDIM, D_MODEL)),
                  _full((N_KEYS, PEER_HALF)), _full((N_KEYS, PEER_HALF))],
        out_specs=(col, col, col, col),
        out_shape=(big(BF16), big(F32), big(F32), big(BF16)),
        compiler_params=_params(("parallel",)),
        name="peer_route",
    )(xnt, wts['wqT'], wts['peer_k1'], wts['peer_k2'])


ROW_TILE = 16
PEER_EB = 1024


def _peer_dense_kernel(xnt_ref, rank_ref, nsel_ref, e1_ref, e2_ref, eu_ref, evt_ref, x1_ref,
                       pe_ref, wple_ref, nple_ref, wgate_ref, nfin_ref,
                       o_ref, yt_ref, u_ref, act_ref, gated_ref, bn_ref, be1_ref, *, eb, final_norm):
    j = pl.program_id(1)

    @pl.when(j == 0)
    def _():
        yt_ref[...] = jnp.zeros_like(yt_ref)
        act_ref[...] = jnp.zeros_like(act_ref)

    xnt = xnt_ref[...]
    u_ref[...] = _dot(eu_ref[...], xnt)
    groups = eb // N_KEYS
    tb = xnt.shape[1]
    zero = jnp.zeros((ROW_TILE, tb), BF16)
    live = jnp.where(j >= 1, 1.0, 0.0)
    prev = jnp.maximum(j - 1, 0)
    for a in range(groups):
        i1 = prev * groups + a
        for h in range(PEER_HEADS):
            bn_ref[a, h] = jnp.broadcast_to(nsel_ref[h, pl.ds(i1, 1), :], (ROW_TILE, tb)).astype(BF16)
            be1_ref[a, h] = jnp.broadcast_to(e1_ref[h, pl.ds(i1, 1), :] * live, (ROW_TILE, tb)).astype(BF16)
        for rt in range(N_KEYS // ROW_TILE):
            rows = slice(rt * ROW_TILE, (rt + 1) * ROW_TILE)
            w = None
            for h in range(PEER_HEADS):
                wh = jnp.where(rank_ref[h, rows, :] < bn_ref[a, h], e2_ref[h, rows, :], zero) * be1_ref[a, h]
                w = wh if w is None else w + wh
            erows = slice(a * N_KEYS + rt * ROW_TILE, a * N_KEYS + (rt + 1) * ROW_TILE)
            gated_ref[erows, :] = w * act_ref[erows, :]
    yt_ref[...] += _dot(evt_ref[...], gated_ref[...])
    u = u_ref[...]
    act_ref[...] = (0.5 * u * (1.0 + lax.erf(u * (2.0 ** -0.5)))).astype(BF16)

    @pl.when(j == pl.num_programs(1) - 1)
    def _():
        x = x1_ref[...] + yt_ref[...].T
        gate = jax.nn.sigmoid(_mm(_rms(x, nple_ref[...], NORM_EPS), wgate_ref[...]))
        x = x + _mm(pe_ref[...], wple_ref[...]) * gate
        o_ref[...] = _rms(x, nfin_ref[...], NORM_EPS) if final_norm else x


def _peer_dense(xnt, route, x1, pe, wts, final_norm):
    t = x1.shape[0]
    tb = _peer_tokens(t)
    eb = PEER_EB
    nblk = N_EXPERTS // eb
    col = pl.BlockSpec((PEER_HEADS, N_KEYS, tb), lambda i, j: (0, 0, i))
    return pl.pallas_call(
        functools.partial(_peer_dense_kernel, eb=eb, final_norm=final_norm),
        grid=(t // tb, nblk + 1),
        in_specs=[pl.BlockSpec((D_MODEL, tb), lambda i, j: (0, i)), col, col, col, col,
                  pl.BlockSpec((eb, D_MODEL), lambda i, j: (jnp.minimum(j, nblk - 1), 0)),
                  pl.BlockSpec((None, D_MODEL, eb), lambda i, j: (jnp.maximum(j - 1, 0), 0, 0)),
                  pl.BlockSpec((tb, D_MODEL), lambda i, j: (i, 0)),
                  pl.BlockSpec((tb, PLE_DIM), lambda i, j: (i, 0)),
                  _full((PLE_DIM, D_MODEL)), _full((1, D_MODEL)), _full((D_MODEL, D_MODEL)),
                  _full((1, D_MODEL))],
        out_specs=pl.BlockSpec((tb, D_MODEL), lambda i, j: (i, 0)),
        out_shape=jax.ShapeDtypeStruct((t, D_MODEL), F32),
        scratch_shapes=[pltpu.VMEM((D_MODEL, tb), F32), pltpu.VMEM((eb, tb), F32),
                        pltpu.VMEM((eb, tb), BF16), pltpu.VMEM((eb, tb), BF16),
                        pltpu.VMEM((eb // N_KEYS, PEER_HEADS, ROW_TILE, tb), BF16),
                        pltpu.VMEM((eb // N_KEYS, PEER_HEADS, ROW_TILE, tb), BF16)],
        compiler_params=_params(("parallel", "arbitrary")),
        name="peer_dense",
    )(xnt, *route, wts['eu'], wts['evT'], x1, pe,
      wts['w_ple'], wts['norm_ple'], wts['w_ple_gate'], wts['norm_final'])


def _token_stages(x, pe, y_rwkv_raw, pre, o_attn, gates, wts, attn_scale, final_norm):
    r, _, kx, v, _, _, g = pre
    x1, xnt = _merge(x, y_rwkv_raw, r, kx, v, g, o_attn, gates, wts, attn_scale)
    return _peer_dense(xnt, _peer_route(xnt, wts), x1, pe, wts, final_norm)


def kernel(x_prompt, x_sample, cache_k, cache_v, state_wkv, state_shift, page_table, p_prompt, p_sample,
           norm_mix, w_in, rwkv_mu, rwkv_w0, rwkv_w2, rwkv_a0, rwkv_a2, rwkv_g2, rwkv_k_k, rwkv_k_a,
           rwkv_r_k, rwkv_ln_w, rwkv_ln_b, attn_lq1, attn_lk1, attn_lq2, attn_lk2, attn_subln, rel_bias,
           w_br_rwkv, w_br_attn, w_out, norm_ffn, peer_wq, peer_k1, peer_k2, peer_u, peer_v,
           norm_ple, w_ple, w_ple_gate, norm_final):
    depth = w_in.shape[0]
    bsz, seq = x_prompt.shape[0], x_prompt.shape[1]
    db, dseq = x_sample.shape[0], x_sample.shape[1]
    assert bsz == 1 and dseq == 1, "prompt batch and decode length are fixed at 1"
    blk = next((b for b in (512, 256) if seq % b == 0), seq)
    xp = x_prompt.reshape(seq, D_MODEL)
    xs = x_sample.reshape(db, D_MODEL)
    tiles, sbias = _bias_tiles(rel_bias, blk)
    seg64 = (jnp.arange(RWKV_W)[:, None] // RWKV_HEAD == jnp.arange(RWKV_W)[None, :] // RWKV_HEAD).astype(BF16)
    row = lambda a: a.reshape(1, -1)
    outs = [[] for _ in range(8)]
    for l in range(depth):
        zpad = jnp.zeros((DECAY_LORA, RWKV_W), F32)
        wts = {
            'mu': row(rwkv_mu[l]), 'w0': row(rwkv_w0[l]), 'a0': row(rwkv_a0[l]),
            'k_k': row(rwkv_k_k[l]), 'k_a': row(rwkv_k_a[l]),
            'w2p': jnp.concatenate([rwkv_w2[l], zpad], axis=0),
            'a2p': jnp.concatenate([zpad, rwkv_a2[l]], axis=0),
            'g2': rwkv_g2[l], 'seg64': seg64,
            'ln_w': row(rwkv_ln_w[l]), 'ln_b': row(rwkv_ln_b[l]), 'r_k': row(rwkv_r_k[l]),
            'subln': row(attn_subln[l]),
            'w_br_rwkv': w_br_rwkv[l].astype(BF16), 'w_br_attn': w_br_attn[l].astype(BF16),
            'w_out': w_out[l].astype(BF16), 'norm_ffn': row(norm_ffn[l]),
            'wqT': peer_wq[l].T.astype(BF16),
            'peer_k1': peer_k1[l], 'peer_k2': peer_k2[l],
            'eu': peer_u[l].astype(BF16),
            'evT': peer_v[l].reshape(N_EXPERTS // PEER_EB, PEER_EB, D_MODEL).transpose(0, 2, 1).astype(BF16),
            'w_ple': w_ple[l].astype(BF16), 'norm_ple': row(norm_ple[l]),
            'w_ple_gate': w_ple_gate[l].astype(BF16), 'norm_final': row(norm_final),
        }
        lam_init = 0.8 - 0.6 * math.exp(-0.3 * l)
        lam = (jnp.exp(jnp.sum(attn_lq1[l] * attn_lk1[l])) - jnp.exp(jnp.sum(attn_lq2[l] * attn_lk2[l]))
               + lam_init).reshape(1).astype(F32)
        w_in_b = w_in[l].astype(BF16)
        g_mix = row(norm_mix[l])

        zr, q, k, v, kb, vt, gates = _proj(xp, g_mix, w_in_b)
        pre = _rwkv_pre(zr, jnp.zeros((1, RWKV_PROJ), F32), False, wts)
        y_raw, wkv_p = _rwkv_chunked(*pre[:6])
        o = _attn_prompt(lam, q, kb, vt, tiles, blk)
        xp = _token_stages(xp, p_prompt[l].reshape(seq, PLE_DIM), y_raw, pre, o, gates, wts,
                           1.0 - lam_init, l == depth - 1)
        outs[0].append(k.reshape(1, seq, ATT_HEADS, 2 * HEAD_QK))
        outs[1].append(v.reshape(1, seq, ATT_HEADS, HEAD_V))
        outs[4].append(wkv_p[None])
        outs[6].append(zr[seq - 1:seq, :])

        zr, q, k, v, _, _, gates = _proj(xs, g_mix, w_in_b)
        pre = _rwkv_pre(zr, state_shift[l], True, wts)
        r, lw, kx, vv, kk, ka, _ = pre
        y_raw, wkv_s = _rwkv_step(state_wkv[l], r, lw, kx, kk, ka, vv)
        o = _attn_sample(lam, q, k, v, cache_k, cache_v, l, page_table, sbias)
        xs = _token_stages(xs, p_sample[l].reshape(db, PLE_DIM), y_raw, pre, o, gates, wts,
                           1.0 - lam_init, l == depth - 1)
        outs[2].append(k.reshape(db, 1, ATT_HEADS, 2 * HEAD_QK))
        outs[3].append(v.reshape(db, 1, ATT_HEADS, HEAD_V))
        outs[5].append(wkv_s)
        outs[7].append(zr)
    y_prompt = xp.reshape(1, seq, D_MODEL)
    y_sample = xs.reshape(db, 1, D_MODEL)
    st = lambda i: jnp.stack(outs[i])
    return (y_prompt, y_sample, st(0), st(1), st(2), st(3), st(4), st(5), st(6), st(7))
```
